```python
import math
import jax, jax.numpy as jnp
from jax import lax
import numpy as np

D_MODEL = 1024
BATCH = 32
SEQ = 2048
DEPTH = 2

N_MEM = 256
MIX_WIDTH = D_MODEL
HGRN_WIDTH = D_MODEL // 2
HGRN_HEAD_DIM = 128
HGRN_HEADS = HGRN_WIDTH // HGRN_HEAD_DIM
HGRN_CHUNK = 64
ATTN_WIDTH = MIX_WIDTH - HGRN_WIDTH
ATTN_HEAD_DIM = 64
ATTN_HEADS = ATTN_WIDTH // ATTN_HEAD_DIM
DILATED_PATTERNS = ((128, 1), (512, 4), (2048, 16))
ATTN_BLOCK = 64
MEM_HEADS = 4
MEM_HEAD_DIM = D_MODEL // MEM_HEADS
D_FF = 2816
FFN_RES = 0.5
EPS = 1e-6
IN_SPLITS = (HGRN_WIDTH, HGRN_WIDTH, HGRN_WIDTH, HGRN_WIDTH, HGRN_WIDTH,
             ATTN_WIDTH, ATTN_WIDTH, ATTN_WIDTH)
IN_COLS = sum(IN_SPLITS)

kernel_name = "hybrid_hgrn2_dilated_attn_macaron_encoder"


def rms_norm(x, g):
    xf = x.astype(jnp.float32)
    y = xf * lax.rsqrt(jnp.mean(xf * xf, axis=-1, keepdims=True) + EPS)
    return (y * g.astype(jnp.float32)).astype(x.dtype)


def swiglu(x, w_gate, w_up, w_down):
    return (jax.nn.silu(x @ w_gate) * (x @ w_up)) @ w_down


def alibi_slopes(n_heads):
    return jnp.asarray(np.array([2.0 ** (-8.0 * (h + 1) / n_heads) for h in range(n_heads)],
                                dtype=np.float32))


def hgrn2_chunk_scan(q, k, v, log_f):
    B, S, H, Dk = q.shape
    Dv = v.shape[-1]
    n_chunks = S // HGRN_CHUNK

    def to_chunks(a):
        return a.reshape(B, n_chunks, HGRN_CHUNK, H, a.shape[-1]).transpose(1, 0, 3, 2, 4)

    xs = tuple(to_chunks(a) for a in (q, k, v, log_f))
    causal_in_chunk = jnp.tril(jnp.ones((HGRN_CHUNK, HGRN_CHUNK), dtype=bool))[:, :, None]

    def step(state, inp):
        qc, kc, vc, gc = inp
        A = jnp.cumsum(gc, axis=2)
        diff = A[:, :, :, None, :] - A[:, :, None, :, :]
        decay = jnp.exp(jnp.where(causal_in_chunk, diff, -jnp.inf))
        scores = jnp.einsum('bhtk,bhsk,bhtsk->bhts', qc, kc, decay)
        o = (jnp.einsum('bhts,bhsv->bhtv', scores, vc)
             + jnp.einsum('bhtk,bhkv->bhtv', qc * jnp.exp(A), state))
        A_last = A[:, :, -1:, :]
        state = (jnp.exp(A_last[:, :, 0, :])[..., None] * state
                 + jnp.einsum('bhsk,bhsv->bhkv', kc * jnp.exp(A_last - A), vc))
        return state, o

    s0 = jnp.zeros((B, H, Dk, Dv), jnp.float32)
    _, o = lax.scan(step, s0, xs)
    return o.transpose(1, 0, 3, 2, 4).reshape(B, S, H, Dv)


def hgrn2_forget(z, lb):
    log_f = jnp.logaddexp(jnp.log(lb), jnp.log1p(-lb) + jax.nn.log_sigmoid(z))
    one_minus_f = (1.0 - lb) * jax.nn.sigmoid(-z)
    return log_f, one_minus_f


def hgrn2_mixer(q, i, z_fwd, z_bwd, g, lb_fwd, lb_bwd, out_gain):
    B, S, _ = q.shape
    heads = lambda a: a.astype(jnp.float32).reshape(B, S, HGRN_HEADS, HGRN_HEAD_DIM)
    qh, ih = heads(q), heads(i)
    lbf = lb_fwd.astype(jnp.float32).reshape(HGRN_HEADS, HGRN_HEAD_DIM)
    lbb = lb_bwd.astype(jnp.float32).reshape(HGRN_HEADS, HGRN_HEAD_DIM)
    logf_f, k_f = hgrn2_forget(heads(z_fwd), lbf)
    logf_b, k_b = hgrn2_forget(heads(z_bwd), lbb)
    o_f = hgrn2_chunk_scan(qh, k_f, ih, logf_f)
    flip = lambda a: jnp.flip(a, axis=1)
    o_b = flip(hgrn2_chunk_scan(flip(qh), flip(k_b), flip(ih), flip(logf_b)))
    o = o_f + o_b
    o = o * lax.rsqrt(jnp.mean(o * o, axis=-1, keepdims=True) + EPS)
    o = o * out_gain.astype(jnp.float32).reshape(HGRN_HEADS, HGRN_HEAD_DIM)
    o = o.reshape(B, S, HGRN_WIDTH) * jax.nn.silu(g.astype(jnp.float32))
    return o.astype(q.dtype)


def dilated_branch(q, k, v, window, dil, slopes):
    B, S, H, Dh = q.shape
    half = window // (2 * dil)
    blk = ATTN_BLOCK
    L = S // dil
    nb = -(-L // blk)
    Lp = nb * blk

    def split(a):
        return a.reshape(B, L, dil, H, Dh).transpose(0, 2, 3, 1, 4)

    qs, ks, vs = split(q), split(k), split(v)
    qb = jnp.pad(qs, ((0, 0), (0, 0), (0, 0), (0, Lp - L), (0, 0))).reshape(B, dil, H, nb, blk, Dh)

    def band(a):
        ap = jnp.pad(a, ((0, 0), (0, 0), (0, 0), (blk, Lp - L + blk), (0, 0)))
        ap = ap.reshape(B, dil, H, nb + 2, blk, Dh)
        return jnp.concatenate([ap[:, :, :, :-2], ap[:, :, :, 1:-1], ap[:, :, :, 2:]], axis=4)

    kw, vw = band(ks), band(vs)
    qpos = jnp.arange(Lp).reshape(nb, blk)
    kpos = (jnp.arange(nb)[:, None] - 1) * blk + jnp.arange(3 * blk)[None, :]
    rel = kpos[:, None, :] - qpos[:, :, None]
    valid = (jnp.abs(rel) <= half) & (kpos[:, None, :] >= 0) & ((kpos[:, None, :] < L) | (rel == 0))
    bias = -(slopes[:, None, None, None] * (dil * jnp.abs(rel)).astype(jnp.float32))

    s = jnp.einsum('brhnqe,brhnke->brhnqk', qb, kw) * (1.0 / math.sqrt(Dh)) + bias[None, None]
    s = jnp.where(valid, s, -jnp.inf)
    m = jnp.max(s, axis=-1, keepdims=True)
    p = jnp.exp(s - m)
    l = jnp.sum(p, axis=-1, keepdims=True)
    o = jnp.einsum('brhnqk,brhnke->brhnqe', p, vw) / l
    lse = (m + jnp.log(l))[..., 0]
    o = o.reshape(B, dil, H, Lp, Dh)[:, :, :, :L].transpose(0, 3, 1, 2, 4).reshape(B, S, H, Dh)
    lse = lse.reshape(B, dil, H, Lp)[:, :, :, :L].transpose(0, 3, 1, 2).reshape(B, S, H)
    return o, lse


def dilated_attention(q, k, v):
    B, S, _ = q.shape
    heads = lambda a: a.astype(jnp.float32).reshape(B, S, ATTN_HEADS, ATTN_HEAD_DIM)
    qh, kh, vh = heads(q), heads(k), heads(v)
    slopes = alibi_slopes(ATTN_HEADS)
    outs, lses = [], []
    for window, dil in DILATED_PATTERNS:
        o, lse = dilated_branch(qh, kh, vh, window, dil, slopes)
        outs.append(o)
        lses.append(lse)
    w = jax.nn.softmax(jnp.stack(lses, axis=0), axis=0)
    o = jnp.sum(w[..., None] * jnp.stack(outs, axis=0), axis=0)
    return o.reshape(B, S, ATTN_WIDTH).astype(q.dtype)


def memory_cross_attention(hn, memn, w_q, w_kv, w_o):
    B, S, _ = hn.shape
    M = memn.shape[1]
    q = (hn @ w_q).astype(jnp.float32).reshape(B, S, MEM_HEADS, MEM_HEAD_DIM)
    k, v = jnp.split((memn @ w_kv).astype(jnp.float32), 2, axis=-1)
    k = k.reshape(B, M, MEM_HEADS, MEM_HEAD_DIM)
    v = v.reshape(B, M, MEM_HEADS, MEM_HEAD_DIM)
    s = jnp.einsum('bshe,bmhe->bhsm', q, k) * (1.0 / math.sqrt(MEM_HEAD_DIM))
    p = jax.nn.softmax(s, axis=-1)
    o = jnp.einsum('bhsm,bmhe->bshe', p, v).reshape(B, S, D_MODEL).astype(hn.dtype)
    return o @ w_o


def setup_inputs(seed: int = 0) -> dict:
    key = jax.random.key(seed)
    ks = jax.random.split(key, 24)
    nrm = lambda k, shape, scale: jax.random.normal(k, shape, jnp.float32) * scale
    gain = lambda k, shape: 1.0 + 0.05 * jax.random.normal(k, shape, jnp.float32)
    D, F = D_MODEL, D_FF
    return {
        "x": nrm(ks[0], (BATCH, SEQ, D), 1.0),
        "mem": nrm(ks[1], (BATCH, N_MEM, D), 1.0),
        "ln_ffn1": gain(ks[2], (DEPTH, D)),
        "ffn1_w_gate": nrm(ks[3], (DEPTH, D, F), D ** -0.5),
        "ffn1_w_up": nrm(ks[4], (DEPTH, D, F), D ** -0.5),
        "ffn1_w_down": nrm(ks[5], (DEPTH, F, D), F ** -0.5),
        "ln_mix": gain(ks[6], (DEPTH, D)),
        "w_in": nrm(ks[7], (DEPTH, D, IN_COLS), D ** -0.5),
        "hgrn_lb_logits": nrm(ks[8], (DEPTH, 2, HGRN_WIDTH), 0.5),
        "hgrn_out_norm": gain(ks[9], (DEPTH, HGRN_WIDTH)),
        "w_out": nrm(ks[10], (DEPTH, MIX_WIDTH, D), MIX_WIDTH ** -0.5),
        "ln_xq": gain(ks[11], (DEPTH, D)),
        "ln_mem": gain(ks[12], (DEPTH, D)),
        "w_xq": nrm(ks[13], (DEPTH, D, D), D ** -0.5),
        "w_xkv": nrm(ks[14], (DEPTH, D, 2 * D), D ** -0.5),
        "w_xo": nrm(ks[15], (DEPTH, D, D), D ** -0.5),
        "ln_ffn2": gain(ks[16], (DEPTH, D)),
        "ffn2_w_gate": nrm(ks[17], (DEPTH, D, F), D ** -0.5),
        "ffn2_w_up": nrm(ks[18], (DEPTH, D, F), D ** -0.5),
        "ffn2_w_down": nrm(ks[19], (DEPTH, F, D), F ** -0.5),
        "ln_final": gain(ks[20], (D,)),
    }


def reference(x, mem, ln_ffn1, ffn1_w_gate, ffn1_w_up, ffn1_w_down, ln_mix, w_in,
              hgrn_lb_logits, hgrn_out_norm, w_out, ln_xq, ln_mem, w_xq, w_xkv, w_xo,
              ln_ffn2, ffn2_w_gate, ffn2_w_up, ffn2_w_down, ln_final):
    lb_all = jnp.cumsum(jax.nn.softmax(hgrn_lb_logits.astype(jnp.float32), axis=0), axis=0)
    lb_all = lb_all - lb_all[0:1]
    offsets = np.cumsum(IN_SPLITS)[:-1].tolist()
    h = x
    for l in range(DEPTH):
        h = h + FFN_RES * swiglu(rms_norm(h, ln_ffn1[l]), ffn1_w_gate[l], ffn1_w_up[l], ffn1_w_down[l])
        u = rms_norm(h, ln_mix[l])
        proj = u @ w_in[l]
        q_h, i_h, zf_h, zb_h, g_h, q_a, k_a, v_a = jnp.split(proj, offsets, axis=-1)
        y_h = hgrn2_mixer(q_h, i_h, zf_h, zb_h, g_h, lb_all[l, 0], lb_all[l, 1], hgrn_out_norm[l])
        y_a = dilated_attention(q_a, k_a, v_a)
        h = h + jnp.concatenate([y_h, y_a], axis=-1) @ w_out[l]
        h = h + memory_cross_attention(rms_norm(h, ln_xq[l]), rms_norm(mem, ln_mem[l]),
                                       w_xq[l], w_xkv[l], w_xo[l])
        h = h + FFN_RES * swiglu(rms_norm(h, ln_ffn2[l]), ffn2_w_gate[l], ffn2_w_up[l], ffn2_w_down[l])
    return rms_norm(h, ln_final)
```

```python
import functools
import math

import numpy as np
import jax
import jax.numpy as jnp
from jax import lax
from jax.experimental import pallas as pl
from jax.experimental.pallas import tpu as pltpu

F32 = jnp.float32
BF16 = jnp.bfloat16

EPS = 1e-6
FFN_RES = 0.5

HGRN_HEAD_DIM = 128
HGRN_HEADS = 4
HGRN_WIDTH = HGRN_HEAD_DIM * HGRN_HEADS
HGRN_PARTS = 5
HGRN_CHUNK = 64
ATTN_HEAD_DIM = 64
ATTN_HEADS = 8
ATTN_WIDTH = ATTN_HEAD_DIM * ATTN_HEADS
ATTN_HALF = 64
ATTN_DILATIONS = (1, 4, 16)
ATTN_QBLK = 128
MEM_HEADS = 4

LANES = 128
VMEM_LIMIT = 56 * 1024 * 1024


def _rms(x, g):
    return x * lax.rsqrt(jnp.mean(x * x, axis=-1, keepdims=True) + EPS) * g


def _dot(a, b):
    return jnp.dot(a, b, preferred_element_type=F32)


def _dot_nt(a, b):
    return lax.dot_general(a, b, (((1,), (1,)), ((), ())), preferred_element_type=F32)


def _dot_tn(a, b):
    return lax.dot_general(a, b, (((0,), (0,)), ((), ())), preferred_element_type=F32)


def _resident(shape):
    return pl.BlockSpec(shape, lambda *_: (0,) * len(shape), pipeline_mode=pl.Buffered(1))


def _ffn_body(x_ref, g_ref, wg_ref, wu_ref, wd_ref, *rest, f_chunk, final_norm):
    o_ref = rest[-1]
    x = x_ref[...]
    xn = _rms(x, g_ref[...]).astype(BF16)
    ffn = None
    for j in range(wg_ref.shape[1] // f_chunk):
        cols = slice(j * f_chunk, (j + 1) * f_chunk)
        gate = _dot(xn, wg_ref[:, cols])
        up = _dot(xn, wu_ref[:, cols])
        act = (gate * jax.nn.sigmoid(gate) * up).astype(BF16)
        part = _dot(act, wd_ref[cols, :])
        ffn = part if ffn is None else ffn + part
    y = x + FFN_RES * ffn
    if final_norm:
        y = _rms(y, rest[0][...])
    o_ref[...] = y


def _ffn(h, g, wg, wu, wd, g_final=None, *, tm=512, f_chunk=1408):
    n, d = h.shape
    f = wg.shape[1]
    final_norm = g_final is not None
    row = pl.BlockSpec((tm, d), lambda i: (i, 0))
    in_specs = [row, _resident((1, d)), _resident((d, f)), _resident((d, f)), _resident((f, d))]
    args = [h, g, wg, wu, wd]
    if final_norm:
        in_specs.append(_resident((1, d)))
        args.append(g_final)
    return pl.pallas_call(
        functools.partial(_ffn_body, f_chunk=f_chunk, final_norm=final_norm),
        grid=(n // tm,),
        in_specs=in_specs,
        out_specs=row,
        out_shape=jax.ShapeDtypeStruct((n, d), F32),
        compiler_params=pltpu.CompilerParams(dimension_semantics=("arbitrary",), vmem_limit_bytes=VMEM_LIMIT),
        name="ffn",
    )(*args)


def _proj_body(x_ref, g_ref, w_ref, oh_ref, oa_ref):
    xn = _rms(x_ref[...], g_ref[...]).astype(BF16)
    n_h = oh_ref.shape[1]
    oh_ref[...] = _dot(xn, w_ref[:, :n_h])
    oa_ref[...] = _dot(xn, w_ref[:, n_h:]).astype(BF16)


def _proj(h, g, w_in, *, tm=512):
    n, d = h.shape
    n_h = HGRN_PARTS * HGRN_WIDTH
    n_a = w_in.shape[1] - n_h
    return pl.pallas_call(
        _proj_body,
        grid=(n // tm,),
        in_specs=[pl.BlockSpec((tm, d), lambda i: (i, 0)), _resident((1, d)), _resident(w_in.shape)],
        out_specs=[pl.BlockSpec((tm, n_h), lambda i: (i, 0)), pl.BlockSpec((tm, n_a), lambda i: (i, 0))],
        out_shape=[jax.ShapeDtypeStruct((n, n_h), F32), jax.ShapeDtypeStruct((n, n_a), BF16)],
        compiler_params=pltpu.CompilerParams(dimension_semantics=("arbitrary",), vmem_limit_bytes=VMEM_LIMIT),
        name="mix_proj",
    )(h, g, w_in)


_HGRN_LEVELS = tuple(2 ** e for e in range(int(math.log2(HGRN_CHUNK))))


def _hgrn_tables(c):
    t = np.arange(c)[:, None]
    s = np.arange(c)[None, :]
    cum = np.stack([(s <= t), (s >= t)]).astype(np.float32)
    masks = np.zeros((2, len(_HGRN_LEVELS) + 1, c, c), np.float32)
    masks[:, 0] = np.eye(c)
    for li, m in enumerate(_HGRN_LEVELS):
        same = (t // (2 * m)) == (s // (2 * m))
        t_hi, s_hi = (t % (2 * m)) >= m, (s % (2 * m)) >= m
        masks[0, li + 1] = same & t_hi & ~s_hi
        masks[1, li + 1] = same & ~t_hi & s_hi
    return jnp.asarray(cum), jnp.asarray(masks)


def _pivot_rows(x, m, fwd):
    c, w = x.shape
    r = m - 1 if fwd else m
    if 2 * m >= 8:
        xb = x.reshape(c // (2 * m), 2 * m, w)
        return jnp.broadcast_to(xb[:, r:r + 1, :], xb.shape).reshape(c, w)
    j = lax.broadcasted_iota(jnp.int32, x.shape, 0) % (2 * m)
    out = x
    for jj in range(2 * m):
        if jj != r:
            out = jnp.where(j == jj, pltpu.roll(x, (jj - r) % c, 0), out)
    return out


def _hgrn_gates(z, log_lb, log1m_lb, one_m_lb):
    log_sig = jnp.minimum(z, 0.0) - jnp.log1p(jnp.exp(-jnp.abs(z)))
    b = log1m_lb + log_sig
    log_f = jnp.maximum(log_lb, b) + jnp.log1p(jnp.exp(-jnp.abs(log_lb - b)))
    return log_f, one_m_lb * jnp.exp(log_sig - z)


def _hgrn_chunk(q, k, v, log_f, state_t, cum, masks, fwd):
    c = q.shape[0]
    x = jnp.dot(cum, log_f, preferred_element_type=F32, precision=lax.Precision.HIGHEST)
    v_bf = v.astype(BF16)
    scores = _dot_nt(q.astype(BF16), k.astype(BF16)) * masks[0]
    for li, m in enumerate(_HGRN_LEVELS):
        pivot = _pivot_rows(x, m, fwd)
        q_m = (q * jnp.exp(jnp.minimum(x - pivot, 0.0))).astype(BF16)
        k_m = (k * jnp.exp(jnp.minimum(pivot - x, 0.0))).astype(BF16)
        scores = scores + _dot_nt(q_m, k_m) * masks[li + 1]
    o = _dot(scores.astype(BF16), v_bf)
    o = o + _dot_nt((q * jnp.exp(x)).astype(BF16), state_t.astype(BF16))
    edge = x[c - 1:c, :] if fwd else x[0:1, :]
    k_out = (k * jnp.exp(edge - x)).astype(BF16)
    state_t = state_t * jnp.exp(edge) + _dot_tn(v_bf, k_out)
    return o, state_t


def _hgrn_body(q_ref, i_ref, zf_ref, zb_ref, g_ref, c_ref, cum_ref, mask_ref, o_ref, acc_ref):
    s, dk = q_ref.shape
    c = HGRN_CHUNK
    n = s // c
    consts = c_ref[...]
    acc_ref[...] = jnp.zeros_like(acc_ref)

    def one(ci, state_t, z_ref, row0, fwd):
        rows = pl.ds(pl.multiple_of(ci * c, c), c)
        log_f, k = _hgrn_gates(z_ref[rows, :], consts[row0:row0 + 1], consts[row0 + 1:row0 + 2],
                               consts[row0 + 2:row0 + 3])
        d = 0 if fwd else 1
        o, state_t = _hgrn_chunk(q_ref[rows, :], k, i_ref[rows, :], log_f, state_t,
                                 cum_ref[d], mask_ref[d], fwd)
        acc_ref[rows, :] += o
        return state_t

    def step(ci, carry):
        st_f, st_b = carry
        return one(ci, st_f, zf_ref, 0, True), one(n - 1 - ci, st_b, zb_ref, 3, False)

    zero = jnp.zeros((dk, dk), F32)
    lax.fori_loop(0, n, step, (zero, zero))

    o = acc_ref[...]
    o = o * lax.rsqrt(jnp.mean(o * o, axis=-1, keepdims=True) + EPS) * consts[6:7]
    g = g_ref[...]
    o_ref[...] = (o * (g * jax.nn.sigmoid(g))).astype(o_ref.dtype)


def _hgrn(p_h, consts, cum, masks):
    b, s, _ = p_h.shape
    dk = HGRN_HEAD_DIM

    def part(p):
        return pl.BlockSpec((None, s, dk), lambda bi, hi, p=p: (bi, 0, p * HGRN_HEADS + hi))

    return pl.pallas_call(
        _hgrn_body,
        grid=(b, HGRN_HEADS),
        in_specs=[part(0), part(1), part(2), part(3), part(4),
                  pl.BlockSpec((8, dk), lambda bi, hi: (0, hi)),
                  _resident(cum.shape), _resident(masks.shape)],
        out_specs=pl.BlockSpec((None, s, dk), lambda bi, hi: (bi, 0, hi)),
        out_shape=jax.ShapeDtypeStruct((b, s, HGRN_WIDTH), BF16),
        scratch_shapes=[pltpu.VMEM((s, dk), F32)],
        compiler_params=pltpu.CompilerParams(dimension_semantics=("arbitrary", "arbitrary"),
                                             vmem_limit_bytes=VMEM_LIMIT),
        name="hgrn2",
    )(p_h, p_h, p_h, p_h, p_h, consts, cum, masks)


def _attn_block(q, k, v, slope_d, rel0, offset):
    rel = rel0 + offset
    dist = jnp.abs(rel)
    s = _dot_nt(q, k) - slope_d * dist
    s = jnp.where(dist <= float(ATTN_HALF), s, -jnp.inf)
    m = jnp.max(s, axis=-1, keepdims=True)
    p = jnp.exp(s - m)
    return m, jnp.sum(p, axis=-1, keepdims=True), _dot(p.astype(BF16), v)


def _attn_body(q_ref, k_ref, v_ref, sl_ref, o_ref, qf, kf, vf, m_ref, l_ref, acc_ref):
    s_len = q_ref.shape[0]
    dh = ATTN_HEAD_DIM
    nq = ATTN_QBLK
    qf[...] = q_ref[...].astype(F32) * (1.0 / math.sqrt(dh))
    kf[...] = k_ref[...].astype(F32)
    vf[...] = v_ref[...].astype(F32)

    for dil in ATTN_DILATIONS:
        length = s_len // dil
        nk = min(2 * nq, length)
        n_blk = length // nq
        rel0 = (lax.broadcasted_iota(jnp.int32, (nq, nk), 1)
                - lax.broadcasted_iota(jnp.int32, (nq, nk), 0)).astype(F32)

        def block(bi, r, dil=dil, length=length, nk=nk, rel0=rel0):
            q0 = bi * nq
            k0 = jnp.clip(q0 - ATTN_HALF, 0, length - nk)
            q_rows = pl.ds(r + dil * q0, nq, stride=dil) if dil > 1 else pl.ds(pl.multiple_of(q0, nq), nq)
            k_rows = pl.ds(r + dil * k0, nk, stride=dil) if dil > 1 else pl.ds(pl.multiple_of(k0, ATTN_HALF), nk)
            qb = qf[q_rows, :].astype(BF16)
            kb = kf[k_rows, :].astype(BF16)
            vb = vf[k_rows, :].astype(BF16)
            offset = (k0 - q0).astype(F32)
            ms, ls, accs = [], [], []
            for h in range(LANES // dh):
                cols = slice(h * dh, (h + 1) * dh)
                slope_d = sl_ref[h:h + 1, :nk] * float(dil)
                m, l, a = _attn_block(qb[:, cols], kb[:, cols], vb[:, cols], slope_d, rel0, offset)
                ms.append(jnp.broadcast_to(m, (nq, dh)))
                ls.append(jnp.broadcast_to(l, (nq, dh)))
                accs.append(a)
            m_new = jnp.concatenate(ms, axis=1)
            l_new = jnp.concatenate(ls, axis=1)
            a_new = jnp.concatenate(accs, axis=1)
            if dil == ATTN_DILATIONS[0]:
                m_ref[q_rows, :] = m_new
                l_ref[q_rows, :] = l_new
                acc_ref[q_rows, :] = a_new
            else:
                m_old = m_ref[q_rows, :]
                m_tot = jnp.maximum(m_old, m_new)
                w_old = jnp.exp(m_old - m_tot)
                w_new = jnp.exp(m_new - m_tot)
                m_ref[q_rows, :] = m_tot
                l_ref[q_rows, :] = l_ref[q_rows, :] * w_old + l_new * w_new
                acc_ref[q_rows, :] = acc_ref[q_rows, :] * w_old + a_new * w_new

        for r in range(dil):
            if n_blk == 1:
                block(jnp.int32(0), r)
            else:
                def loop_body(bi, carry, r=r):
                    block(bi, r)
                    return carry
                lax.fori_loop(0, n_blk, loop_body, 0)

    o_ref[...] = (acc_ref[...] / l_ref[...]).astype(o_ref.dtype)


def _attn(p_a, slopes):
    b, s, _ = p_a.shape
    pairs = ATTN_WIDTH // LANES

    def part(p):
        return pl.BlockSpec((None, s, LANES), lambda bi, hi, p=p: (bi, 0, p * pairs + hi))

    return pl.pallas_call(
        _attn_body,
        grid=(b, pairs),
        in_specs=[part(0), part(1), part(2),
                  pl.BlockSpec((None, 8, 2 * ATTN_QBLK), lambda bi, hi: (hi, 0, 0))],
        out_specs=pl.BlockSpec((None, s, LANES), lambda bi, hi: (bi, 0, hi)),
        out_shape=jax.ShapeDtypeStruct((b, s, ATTN_WIDTH), BF16),
        scratch_shapes=[pltpu.VMEM((s, LANES), F32)] * 6,
        compiler_params=pltpu.CompilerParams(dimension_semantics=("arbitrary", "arbitrary"),
                                             vmem_limit_bytes=VMEM_LIMIT),
        name="dilated_attn",
    )(p_a, p_a, p_a, slopes)


def _memkv_body(m_ref, g_ref, w_ref, k_ref, v_ref):
    mn = _rms(m_ref[...], g_ref[...]).astype(BF16)
    d = k_ref.shape[1]
    k_ref[...] = _dot(mn, w_ref[:, :d]).astype(BF16)
    v_ref[...] = _dot(mn, w_ref[:, d:]).astype(BF16)


def _memkv(mem, g, w_kv):
    b, m, d = mem.shape
    blk = pl.BlockSpec((None, m, d), lambda bi: (bi, 0, 0))
    return pl.pallas_call(
        _memkv_body,
        grid=(b,),
        in_specs=[blk, _resident((1, d)), _resident(w_kv.shape)],
        out_specs=[blk, blk],
        out_shape=[jax.ShapeDtypeStruct((b, m, d), BF16)] * 2,
        compiler_params=pltpu.CompilerParams(dimension_semantics=("arbitrary",), vmem_limit_bytes=VMEM_LIMIT),
        name="mem_kv",
    )(mem, g, w_kv)


def _mix_xattn_body(h_ref, yh_ref, ya_ref, wo_h_ref, wo_a_ref, g_ref, wq_ref, k_ref, v_ref, wxo_ref, o_ref):
    h = h_ref[...] + _dot(yh_ref[...], wo_h_ref[...]) + _dot(ya_ref[...], wo_a_ref[...])
    d = h.shape[1]
    dh = d // MEM_HEADS
    hn = _rms(h, g_ref[...]).astype(BF16)
    q = (_dot(hn, wq_ref[...]) * (1.0 / math.sqrt(dh))).astype(BF16)
    outs = []
    for hd in range(MEM_HEADS):
        cols = slice(hd * dh, (hd + 1) * dh)
        s = _dot_nt(q[:, cols], k_ref[:, cols])
        p = jnp.exp(s - jnp.max(s, axis=-1, keepdims=True))
        l = jnp.sum(p, axis=-1, keepdims=True)
        outs.append((_dot(p.astype(BF16), v_ref[:, cols]) / l).astype(BF16))
    o_ref[...] = h + _dot(jnp.concatenate(outs, axis=1), wxo_ref[...])


def _mix_xattn(h, y_h, y_a, wo_h, wo_a, g, wq, k, v, wxo, *, tm=512):
    b, s, d = h.shape
    m = k.shape[1]

    def rows(w):
        return pl.BlockSpec((None, tm, w), lambda bi, i: (bi, i, 0))

    mem = pl.BlockSpec((None, m, d), lambda bi, i: (bi, 0, 0))
    return pl.pallas_call(
        _mix_xattn_body,
        grid=(b, s // tm),
        in_specs=[rows(d), rows(y_h.shape[2]), rows(y_a.shape[2]), _resident(wo_h.shape), _resident(wo_a.shape),
                  _resident((1, d)), _resident(wq.shape), mem, mem, _resident(wxo.shape)],
        out_specs=rows(d),
        out_shape=jax.ShapeDtypeStruct((b, s, d), F32),
        compiler_params=pltpu.CompilerParams(dimension_semantics=("arbitrary", "arbitrary"),
                                             vmem_limit_bytes=VMEM_LIMIT),
        name="mix_out_xattn",
    )(h, y_h, y_a, wo_h, wo_a, g, wq, k, v, wxo)


def _hgrn_consts(lb_fwd, lb_bwd, out_gain):
    rows = []
    for lb in (lb_fwd, lb_bwd):
        rows += [jnp.log(lb), jnp.log1p(-lb), 1.0 - lb]
    rows += [out_gain, jnp.zeros_like(out_gain)]
    return jnp.stack(rows).astype(F32)


def _attn_slopes():
    sl = np.array([2.0 ** (-8.0 * (h + 1) / ATTN_HEADS) for h in range(ATTN_HEADS)], np.float32)
    per_pair = LANES // ATTN_HEAD_DIM
    out = np.zeros((ATTN_HEADS // per_pair, 8, 2 * ATTN_QBLK), np.float32)
    for h in range(ATTN_HEADS):
        out[h // per_pair, h % per_pair, :] = sl[h]
    return jnp.asarray(out)


def kernel(x, mem, ln_ffn1, ffn1_w_gate, ffn1_w_up, ffn1_w_down, ln_mix, w_in, hgrn_lb_logits, hgrn_out_norm,
           w_out, ln_xq, ln_mem, w_xq, w_xkv, w_xo, ln_ffn2, ffn2_w_gate, ffn2_w_up, ffn2_w_down, ln_final):
    b, s, d = x.shape
    depth = ln_ffn1.shape[0]
    assert s % (ATTN_DILATIONS[-1] * ATTN_QBLK) == 0 and s % HGRN_CHUNK == 0

    lb_all = jnp.cumsum(jax.nn.softmax(hgrn_lb_logits.astype(F32), axis=0), axis=0)
    lb_all = lb_all - lb_all[0:1]
    cum, masks = _hgrn_tables(HGRN_CHUNK)
    slopes = _attn_slopes()
    bf = lambda w: w.astype(BF16)
    vec = lambda g: g.astype(F32).reshape(1, -1)

    h = x.reshape(b * s, d)
    for l in range(depth):
        h = _ffn(h, vec(ln_ffn1[l]), bf(ffn1_w_gate[l]), bf(ffn1_w_up[l]), bf(ffn1_w_down[l]))
        p_h, p_a = _proj(h, vec(ln_mix[l]), bf(w_in[l]))
        y_h = _hgrn(p_h.reshape(b, s, -1), _hgrn_consts(lb_all[l, 0], lb_all[l, 1], hgrn_out_norm[l]), cum, masks)
        y_a = _attn(p_a.reshape(b, s, -1), slopes)
        k_mem, v_mem = _memkv(mem, vec(ln_mem[l]), bf(w_xkv[l]))
        wo = bf(w_out[l])
        h = _mix_xattn(h.reshape(b, s, d), y_h, y_a, wo[:HGRN_WIDTH], wo[HGRN_WIDTH:], vec(ln_xq[l]),
                       bf(w_xq[l]), k_mem, v_mem, bf(w_xo[l])).reshape(b * s, d)
        last = l == depth - 1
        h = _ffn(h, vec(ln_ffn2[l]), bf(ffn2_w_gate[l]), bf(ffn2_w_up[l]), bf(ffn2_w_down[l]),
                 vec(ln_final) if last else None)
    return h.reshape(b, s, d)
```

```python
import functools
import math

import numpy as np
import jax
import jax.numpy as jnp
from jax import lax
from jax.experimental import pallas as pl
from jax.experimental.pallas import tpu as pltpu

F32 = jnp.float32
BF16 = jnp.bfloat16

EPS = 1e-6
FFN_RES = 0.5

HGRN_HEAD_DIM = 128
HGRN_HEADS = 4
HGRN_WIDTH = HGRN_HEAD_DIM * HGRN_HEADS
HGRN_PARTS = 5
HGRN_CHUNK = 64
HGRN_UNROLL = 4
ATTN_HEAD_DIM = 64
ATTN_HEADS = 8
ATTN_WIDTH = ATTN_HEAD_DIM * ATTN_HEADS
ATTN_HALF = 64
ATTN_DILATIONS = (1, 4, 16)
ATTN_QBLK = 128
ATTN_BLOCKS_PER_BODY = 2
MEM_HEADS = 4

LANES = 128
VMEM_LIMIT = 56 * 1024 * 1024


def _rms(x, g):
    return x * lax.rsqrt(jnp.mean(x * x, axis=-1, keepdims=True) + EPS) * g


def _dot(a, b):
    return jnp.dot(a, b, preferred_element_type=F32)


def _dot_nt(a, b):
    return lax.dot_general(a, b, (((1,), (1,)), ((), ())), preferred_element_type=F32)


def _dot_tn(a, b):
    return lax.dot_general(a, b, (((0,), (0,)), ((), ())), preferred_element_type=F32)


def _resident(shape):
    return pl.BlockSpec(shape, lambda *_: (0,) * len(shape), pipeline_mode=pl.Buffered(1))


def _ffn_body(x_ref, g_ref, wg_ref, wu_ref, wd_ref, *rest, f_chunk, final_norm):
    o_ref = rest[-1]
    x = x_ref[...]
    xn = _rms(x, g_ref[...]).astype(BF16)
    ffn = None
    for j in range(wg_ref.shape[1] // f_chunk):
        cols = slice(j * f_chunk, (j + 1) * f_chunk)
        gate = _dot(xn, wg_ref[:, cols])
        up = _dot(xn, wu_ref[:, cols])
        act = (gate * jax.nn.sigmoid(gate) * up).astype(BF16)
        part = _dot(act, wd_ref[cols, :])
        ffn = part if ffn is None else ffn + part
    y = x + FFN_RES * ffn
    if final_norm:
        y = _rms(y, rest[0][...])
    o_ref[...] = y


def _ffn(h, g, wg, wu, wd, g_final=None, *, tm=512, f_chunk=1408):
    n, d = h.shape
    f = wg.shape[1]
    final_norm = g_final is not None
    row = pl.BlockSpec((tm, d), lambda i: (i, 0))
    in_specs = [row, _resident((1, d)), _resident((d, f)), _resident((d, f)), _resident((f, d))]
    args = [h, g, wg, wu, wd]
    if final_norm:
        in_specs.append(_resident((1, d)))
        args.append(g_final)
    return pl.pallas_call(
        functools.partial(_ffn_body, f_chunk=f_chunk, final_norm=final_norm),
        grid=(n // tm,),
        in_specs=in_specs,
        out_specs=row,
        out_shape=jax.ShapeDtypeStruct((n, d), F32),
        compiler_params=pltpu.CompilerParams(dimension_semantics=("arbitrary",), vmem_limit_bytes=VMEM_LIMIT),
        name="ffn",
    )(*args)


ATTN_SLABS = 5
_Q0, _Q1, _K, _V0, _V1 = range(ATTN_SLABS)


def _proj_body(x_ref, g_ref, w_ref, oh_ref, oa_ref):
    xn = _rms(x_ref[...], g_ref[...]).astype(BF16)
    n_h = oh_ref.shape[1]
    oh_ref[...] = _dot(xn, w_ref[:, :n_h])
    pa = _dot(xn, w_ref[:, n_h:])
    tm = pa.shape[0]
    lane = lax.broadcasted_iota(jnp.int32, (tm, LANES), 1)
    head0 = lane < ATTN_HEAD_DIM
    for hp in range(ATTN_WIDTH // LANES):
        q, k, v = (pa[:, p * ATTN_WIDTH + hp * LANES: p * ATTN_WIDTH + (hp + 1) * LANES] for p in range(3))
        q = q * (1.0 / math.sqrt(ATTN_HEAD_DIM))
        slabs = {
            _Q0: jnp.where(head0, q, 0.0),
            _Q1: jnp.where(head0, 0.0, q),
            _K: k,
            _V0: jnp.where(head0, v, 1.0),
            _V1: jnp.where(head0, 1.0, v),
        }
        for i, val in slabs.items():
            oa_ref[hp, :, i * LANES:(i + 1) * LANES] = val.astype(BF16)


def _proj(h, g, w_in, *, tm=512):
    b, s, d = h.shape
    n_h = HGRN_PARTS * HGRN_WIDTH
    pairs = ATTN_WIDTH // LANES
    return pl.pallas_call(
        _proj_body,
        grid=(b, s // tm),
        in_specs=[pl.BlockSpec((None, tm, d), lambda bi, i: (bi, i, 0)), _resident((1, d)), _resident(w_in.shape)],
        out_specs=[pl.BlockSpec((None, tm, n_h), lambda bi, i: (bi, i, 0)),
                   pl.BlockSpec((None, pairs, tm, ATTN_SLABS * LANES), lambda bi, i: (bi, 0, i, 0))],
        out_shape=[jax.ShapeDtypeStruct((b, s, n_h), F32),
                   jax.ShapeDtypeStruct((b, pairs, s, ATTN_SLABS * LANES), BF16)],
        compiler_params=pltpu.CompilerParams(dimension_semantics=("arbitrary", "arbitrary"),
                                             vmem_limit_bytes=VMEM_LIMIT),
        name="mix_proj",
    )(h, g, w_in)


_HGRN_LEVELS = tuple(2 ** e for e in range(int(math.log2(HGRN_CHUNK))))


def _hgrn_tables(c):
    t = np.arange(c)[:, None]
    s = np.arange(c)[None, :]
    cum = np.stack([(s <= t), (s >= t)]).astype(np.float32)
    cum = jnp.asarray(cum, BF16)
    masks = np.zeros((2, len(_HGRN_LEVELS) + 1, c, c), np.float32)
    masks[:, 0] = np.eye(c)
    for li, m in enumerate(_HGRN_LEVELS):
        same = (t // (2 * m)) == (s // (2 * m))
        t_hi, s_hi = (t % (2 * m)) >= m, (s % (2 * m)) >= m
        masks[0, li + 1] = same & t_hi & ~s_hi
        masks[1, li + 1] = same & ~t_hi & s_hi
    return jnp.asarray(cum), jnp.asarray(masks)


def _pivot_rows(x, m, fwd):
    c, w = x.shape
    r = m - 1 if fwd else m
    if 2 * m >= 8:
        xb = x.reshape(c // (2 * m), 2 * m, w)
        return jnp.broadcast_to(xb[:, r:r + 1, :], xb.shape).reshape(c, w)
    j = lax.broadcasted_iota(jnp.int32, x.shape, 0) % (2 * m)
    out = x
    for jj in range(2 * m):
        if jj != r:
            out = jnp.where(j == jj, pltpu.roll(x, (jj - r) % c, 0), out)
    return out


def _hgrn_gates(z, log_lb, log1m_lb, one_m_lb, one):
    log_sig = jnp.minimum(z, 0.0) - jnp.log(one + jnp.exp(-jnp.abs(z)))
    b = log1m_lb + log_sig
    log_f = jnp.maximum(log_lb, b) + jnp.log(one + jnp.exp(-jnp.abs(log_lb - b)))
    return log_f, one_m_lb * jnp.exp(log_sig - z)


def _cumsum_rows(cum_bf, x):
    w = x.shape[1]
    hi = x.astype(BF16)
    rest = x - hi.astype(F32)
    mid = rest.astype(BF16)
    lo = (rest - mid.astype(F32)).astype(BF16)
    y = _dot(cum_bf, jnp.concatenate([hi, mid, lo], axis=1))
    return y[:, :w] + y[:, w:2 * w] + y[:, 2 * w:]


def _hgrn_group(items, states, cum_ref, mask_ref):
    c = HGRN_CHUNK
    xs = [_cumsum_rows(cum_ref[d], log_f) for _, _, _, log_f, d in items]
    v_bf = [v.astype(BF16) for _, _, v, _, _ in items]
    masks = [mask_ref[d, 0] for d in (0, 1)]
    scores = [_dot_nt(q.astype(BF16), k.astype(BF16)) * masks[d] for q, k, _, _, d in items]
    for li, m in enumerate(_HGRN_LEVELS):
        masks = [mask_ref[d, li + 1] for d in (0, 1)]
        for j, (q, k, _, _, d) in enumerate(items):
            e = jnp.exp(-jnp.abs(xs[j] - _pivot_rows(xs[j], m, d == 0)))
            scores[j] = scores[j] + _dot_nt((q * e).astype(BF16), (k * e).astype(BF16)) * masks[d]
    outs = [_dot(sc.astype(BF16), v) for sc, v in zip(scores, v_bf)]
    edges = [x[c - 1:c, :] if d == 0 else x[0:1, :] for x, (_, _, _, _, d) in zip(xs, items)]
    handed = [_dot_tn(v, (k * jnp.exp(edge - x)).astype(BF16))
              for v, edge, x, (_, k, _, _, _) in zip(v_bf, edges, xs, items)]
    q_in = [(q * jnp.exp(x)).astype(BF16) for x, (q, _, _, _, _) in zip(xs, items)]
    states = list(states)
    for j, (_, _, _, _, d) in enumerate(items):
        outs[j] = outs[j] + _dot_nt(q_in[j], states[d].astype(BF16))
        states[d] = states[d] * jnp.exp(edges[j]) + handed[j]
    return outs, states


def _hgrn_body(q_ref, i_ref, zf_ref, zb_ref, g_ref, c_ref, cum_ref, mask_ref, o_ref, acc_ref):
    s, dk = q_ref.shape
    c = HGRN_CHUNK
    n = s // c
    consts = c_ref[...]
    acc_ref[...] = jnp.zeros_like(acc_ref)
    z_refs = (zf_ref, zb_ref)

    def step(gi, states):
        rows, items = [], []
        for u in range(HGRN_UNROLL):
            for d in (0, 1):
                ci = gi * HGRN_UNROLL + u
                r = pl.ds(pl.multiple_of((ci if d == 0 else n - 1 - ci) * c, c), c)
                log_f, k = _hgrn_gates(z_refs[d][r, :], *(consts[3 * d + i:3 * d + i + 1] for i in range(3)),
                                       consts[7:8])
                rows.append(r)
                items.append((q_ref[r, :], k, i_ref[r, :], log_f, d))
        outs, states = _hgrn_group(items, states, cum_ref, mask_ref)
        for r, o in zip(rows, outs):
            acc_ref[r, :] += o
        return tuple(states)

    zero = jnp.zeros((dk, dk), F32)
    lax.fori_loop(0, n // HGRN_UNROLL, step, (zero, zero))

    o = acc_ref[...]
    o = o * lax.rsqrt(jnp.mean(o * o, axis=-1, keepdims=True) + EPS) * consts[6:7]
    g = g_ref[...]
    o_ref[...] = (o * (g * jax.nn.sigmoid(g))).astype(o_ref.dtype)


def _hgrn(p_h, consts, cum, masks):
    b, s, _ = p_h.shape
    dk = HGRN_HEAD_DIM

    def part(p):
        return pl.BlockSpec((None, s, dk), lambda bi, hi, p=p: (bi, 0, p * HGRN_HEADS + hi))

    return pl.pallas_call(
        _hgrn_body,
        grid=(b, HGRN_HEADS),
        in_specs=[part(0), part(1), part(2), part(3), part(4),
                  pl.BlockSpec((8, dk), lambda bi, hi: (0, hi)),
                  _resident(cum.shape), _resident(masks.shape)],
        out_specs=pl.BlockSpec((None, s, dk), lambda bi, hi: (bi, 0, hi)),
        out_shape=jax.ShapeDtypeStruct((b, s, HGRN_WIDTH), BF16),
        scratch_shapes=[pltpu.VMEM((s, dk), F32)],
        compiler_params=pltpu.CompilerParams(dimension_semantics=("arbitrary", "arbitrary"),
                                             vmem_limit_bytes=VMEM_LIMIT),
        name="hgrn2",
    )(p_h, p_h, p_h, p_h, p_h, consts, cum, masks)


def _attn_geometry(s_len, dil):
    length = s_len // dil
    nq = min(ATTN_QBLK, length)
    nk = min(nq + 2 * ATTN_HALF, length)
    return length, nq, nk, length // nq


def _attn_body(x1_ref, x4_ref, x16_ref, sl_ref, o_ref, bias_s, m_s, acc_s):
    x_refs = (x1_ref, x4_ref, x16_ref)
    s_len = x1_ref.shape[0]
    heads = LANES // ATTN_HEAD_DIM
    slab_w = ATTN_SLABS * LANES

    @pl.when(pl.program_id(1) == 0)
    def _():
        for di, dil in enumerate(ATTN_DILATIONS):
            _, nq, nk, _ = _attn_geometry(s_len, dil)
            rel0 = lax.broadcasted_iota(jnp.int32, (nq, nk), 1) - lax.broadcasted_iota(jnp.int32, (nq, nk), 0)
            for pos, offset in enumerate((0, -ATTN_HALF, nq - nk)):
                dist = jnp.abs(rel0 + offset).astype(F32)
                for h in range(heads):
                    pen = -(sl_ref[h:h + 1, :nk] * (float(dil) * dist))
                    bias_s[di, pos, h, :nq, :nk] = jnp.where(dist <= float(ATTN_HALF), pen, -jnp.inf)

    def slab(item, rows, i):
        di, _, _, r = item
        return x_refs[di][rows, r * slab_w + i * LANES: r * slab_w + (i + 1) * LANES]

    def rows_of(item):
        _, dil, bi, r = item
        length, nq, nk, n_blk = _attn_geometry(s_len, dil)
        q0 = bi * nq
        k0 = min(max(q0 - ATTN_HALF, 0), length - nk)
        pos = 0 if bi == 0 else (2 if bi == n_blk - 1 else 1)
        out_rows = pl.ds(r + dil * q0, nq, stride=dil) if dil > 1 else pl.ds(q0, nq)
        return pl.ds(q0, nq), pl.ds(k0, nk), out_rows, pos, nq, nk

    def scores(item):
        q_rows, k_rows, _, pos, nq, nk = rows_of(item)
        kb = slab(item, k_rows, _K)
        return [_dot_nt(slab(item, q_rows, (_Q0, _Q1)[h]), kb) + bias_s[item[0], pos, h, :nq, :nk]
                for h in range(heads)]

    def probs(s):
        m = jnp.max(s, axis=-1, keepdims=True)
        return m, jnp.exp(s - m).astype(BF16)

    def finish(item, h, m, p):
        _, k_rows, out_rows, _, nq, _ = rows_of(item)
        m_s[item[0], h, out_rows, :] = jnp.broadcast_to(m, (nq, LANES))
        acc_s[item[0], h, out_rows, :] = _dot(p, slab(item, k_rows, (_V0, _V1)[h]))

    items = [(di, dil, bi, r) for di, dil in enumerate(ATTN_DILATIONS)
             for bi in range(_attn_geometry(s_len, dil)[3]) for r in range(dil)]
    groups = [items[i:i + ATTN_BLOCKS_PER_BODY] for i in range(0, len(items), ATTN_BLOCKS_PER_BODY)]
    pending = None
    for group in groups + [None]:
        upcoming = [(item, scores(item)) for item in group] if group else None
        if pending:
            for item, ss in pending:
                for h, s in enumerate(ss):
                    finish(item, h, *probs(s))
        pending = upcoming

    rows_per_step = ATTN_QBLK
    head0 = lax.broadcasted_iota(jnp.int32, (rows_per_step, LANES), 1) < ATTN_HEAD_DIM

    def merge(ti, carry):
        rows = pl.ds(pl.multiple_of(ti * rows_per_step, rows_per_step), rows_per_step)
        acc = []
        for h in range(heads):
            ms = [m_s[di, h, rows, :] for di in range(len(ATTN_DILATIONS))]
            m_tot = functools.reduce(jnp.maximum, ms)
            acc.append(sum(jnp.exp(m - m_tot) * acc_s[di, h, rows, :] for di, m in enumerate(ms)))
        num = jnp.where(head0, acc[0], acc[1])
        den = pltpu.roll(jnp.where(head0, acc[1], acc[0]), ATTN_HEAD_DIM, 1)
        o_ref[rows, :] = (num / den).astype(o_ref.dtype)
        return carry

    lax.fori_loop(0, s_len // rows_per_step, merge, 0)


def _attn(p_a, slopes):
    b, pairs, s, slab_w = p_a.shape
    heads = LANES // ATTN_HEAD_DIM
    n_dil = len(ATTN_DILATIONS)
    nk_max = ATTN_QBLK + 2 * ATTN_HALF

    def view(dil):
        return p_a.reshape(b, pairs, s // dil, dil * slab_w)

    def view_spec(dil):
        return pl.BlockSpec((None, None, s // dil, dil * slab_w), lambda hi, bi: (bi, hi, 0, 0))

    return pl.pallas_call(
        _attn_body,
        grid=(pairs, b),
        in_specs=[view_spec(d) for d in ATTN_DILATIONS]
                 + [pl.BlockSpec((None, 8, nk_max), lambda hi, bi: (hi, 0, 0))],
        out_specs=pl.BlockSpec((None, s, LANES), lambda hi, bi: (bi, 0, hi)),
        out_shape=jax.ShapeDtypeStruct((b, s, ATTN_WIDTH), BF16),
        scratch_shapes=[pltpu.VMEM((n_dil, 3, heads, ATTN_QBLK, nk_max), F32),
                        pltpu.VMEM((n_dil, heads, s, LANES), F32),
                        pltpu.VMEM((n_dil, heads, s, LANES), F32)],
        compiler_params=pltpu.CompilerParams(dimension_semantics=("arbitrary", "arbitrary"),
                                             vmem_limit_bytes=VMEM_LIMIT),
        name="dilated_attn",
    )(*[view(d) for d in ATTN_DILATIONS], slopes)


def _memkv_body(m_ref, g_ref, w_ref, k_ref, v_ref):
    mn = _rms(m_ref[...], g_ref[...]).astype(BF16)
    d = k_ref.shape[1]
    k_ref[...] = _dot(mn, w_ref[:, :d]).astype(BF16)
    v_ref[...] = _dot(mn, w_ref[:, d:]).astype(BF16)


def _memkv(mem, g, w_kv):
    b, m, d = mem.shape
    blk = pl.BlockSpec((None, m, d), lambda bi: (bi, 0, 0))
    return pl.pallas_call(
        _memkv_body,
        grid=(b,),
        in_specs=[blk, _resident((1, d)), _resident(w_kv.shape)],
        out_specs=[blk, blk],
        out_shape=[jax.ShapeDtypeStruct((b, m, d), BF16)] * 2,
        compiler_params=pltpu.CompilerParams(dimension_semantics=("arbitrary",), vmem_limit_bytes=VMEM_LIMIT),
        name="mem_kv",
    )(mem, g, w_kv)


def _mix_xattn_body(h_ref, yh_ref, ya_ref, wo_h_ref, wo_a_ref, g_ref, wq_ref, k_ref, v_ref, wxo_ref, o_ref):
    h = h_ref[...] + _dot(yh_ref[...], wo_h_ref[...]) + _dot(ya_ref[...], wo_a_ref[...])
    d = h.shape[1]
    dh = d // MEM_HEADS
    hn = _rms(h, g_ref[...]).astype(BF16)
    q = (_dot(hn, wq_ref[...]) * (1.0 / math.sqrt(dh))).astype(BF16)
    outs = []
    for hd in range(MEM_HEADS):
        cols = slice(hd * dh, (hd + 1) * dh)
        s = _dot_nt(q[:, cols], k_ref[:, cols])
        p = jnp.exp(s - jnp.max(s, axis=-1, keepdims=True))
        l = jnp.sum(p, axis=-1, keepdims=True)
        outs.append((_dot(p.astype(BF16), v_ref[:, cols]) / l).astype(BF16))
    o_ref[...] = h + _dot(jnp.concatenate(outs, axis=1), wxo_ref[...])


def _mix_xattn(h, y_h, y_a, wo_h, wo_a, g, wq, k, v, wxo, *, tm=512):
    b, s, d = h.shape
    m = k.shape[1]

    def rows(w):
        return pl.BlockSpec((None, tm, w), lambda bi, i: (bi, i, 0))

    mem = pl.BlockSpec((None, m, d), lambda bi, i: (bi, 0, 0))
    return pl.pallas_call(
        _mix_xattn_body,
        grid=(b, s // tm),
        in_specs=[rows(d), rows(y_h.shape[2]), rows(y_a.shape[2]), _resident(wo_h.shape), _resident(wo_a.shape),
                  _resident((1, d)), _resident(wq.shape), mem, mem, _resident(wxo.shape)],
        out_specs=rows(d),
        out_shape=jax.ShapeDtypeStruct((b, s, d), F32),
        compiler_params=pltpu.CompilerParams(dimension_semantics=("arbitrary", "arbitrary"),
                                             vmem_limit_bytes=VMEM_LIMIT),
        name="mix_out_xattn",
    )(h, y_h, y_a, wo_h, wo_a, g, wq, k, v, wxo)


def _hgrn_consts(lb_fwd, lb_bwd, out_gain):
    rows = []
    for lb in (lb_fwd, lb_bwd):
        rows += [jnp.log(lb), jnp.log1p(-lb), 1.0 - lb]
    rows += [out_gain, jnp.ones_like(out_gain)]
    return jnp.stack(rows).astype(F32)


def _attn_slopes():
    sl = np.array([2.0 ** (-8.0 * (h + 1) / ATTN_HEADS) for h in range(ATTN_HEADS)], np.float32)
    per_pair = LANES // ATTN_HEAD_DIM
    out = np.zeros((ATTN_HEADS // per_pair, 8, ATTN_QBLK + 2 * ATTN_HALF), np.float32)
    for h in range(ATTN_HEADS):
        out[h // per_pair, h % per_pair, :] = sl[h]
    return jnp.asarray(out)


def kernel(x, mem, ln_ffn1, ffn1_w_gate, ffn1_w_up, ffn1_w_down, ln_mix, w_in, hgrn_lb_logits, hgrn_out_norm,
           w_out, ln_xq, ln_mem, w_xq, w_xkv, w_xo, ln_ffn2, ffn2_w_gate, ffn2_w_up, ffn2_w_down, ln_final):
    b, s, d = x.shape
    depth = ln_ffn1.shape[0]
    assert s % (ATTN_DILATIONS[-1] * 2 * ATTN_HALF) == 0 and s % ATTN_QBLK == 0 and s % HGRN_CHUNK == 0

    lb_all = jnp.cumsum(jax.nn.softmax(hgrn_lb_logits.astype(F32), axis=0), axis=0)
    lb_all = lb_all - lb_all[0:1]
    cum, masks = _hgrn_tables(HGRN_CHUNK)
    slopes = _attn_slopes()
    bf = lambda w: w.astype(BF16)
    vec = lambda g: g.astype(F32).reshape(1, -1)

    h = x.reshape(b * s, d)
    for l in range(depth):
        h = _ffn(h, vec(ln_ffn1[l]), bf(ffn1_w_gate[l]), bf(ffn1_w_up[l]), bf(ffn1_w_down[l]))
        p_h, p_a = _proj(h.reshape(b, s, d), vec(ln_mix[l]), bf(w_in[l]))
        y_h = _hgrn(p_h, _hgrn_consts(lb_all[l, 0], lb_all[l, 1], hgrn_out_norm[l]), cum, masks)
        y_a = _attn(p_a, slopes)
        k_mem, v_mem = _memkv(mem, vec(ln_mem[l]), bf(w_xkv[l]))
        wo = bf(w_out[l])
        h = _mix_xattn(h.reshape(b, s, d), y_h, y_a, wo[:HGRN_WIDTH], wo[HGRN_WIDTH:], vec(ln_xq[l]),
                       bf(w_xq[l]), k_mem, v_mem, bf(w_xo[l])).reshape(b * s, d)
        last = l == depth - 1
        h = _ffn(h, vec(ln_ffn2[l]), bf(ffn2_w_gate[l]), bf(ffn2_w_up[l]), bf(ffn2_w_down[l]),
                 vec(ln_final) if last else None)
    return h.reshape(b, s, d)
```

```python
import functools
import math

import numpy as np
import jax
import jax.numpy as jnp
from jax import lax
from jax.experimental import pallas as pl
from jax.experimental.pallas import tpu as pltpu

F32 = jnp.float32
BF16 = jnp.bfloat16

EPS = 1e-6
FFN_RES = 0.5
LOG2_E = math.log2(math.e)

HGRN_HEAD_DIM = 128
HGRN_HEADS = 4
HGRN_WIDTH = HGRN_HEAD_DIM * HGRN_HEADS
HGRN_PARTS = 5
HGRN_CHUNK = 64
HGRN_UNROLL = 4
ATTN_HEAD_DIM = 64
ATTN_HEADS = 8
ATTN_WIDTH = ATTN_HEAD_DIM * ATTN_HEADS
ATTN_HALF = 64
ATTN_DILATIONS = (1, 4, 16)
ATTN_QBLK = 128
ATTN_BLOCKS_PER_BODY = 2
MEM_HEADS = 4

LANES = 128
VMEM_LIMIT = 56 * 1024 * 1024


def _rms(x, g):
    return x * lax.rsqrt(jnp.mean(x * x, axis=-1, keepdims=True) + EPS) * g


def _dot(a, b):
    return jnp.dot(a, b, preferred_element_type=F32)


def _dot_nt(a, b):
    return lax.dot_general(a, b, (((1,), (1,)), ((), ())), preferred_element_type=F32)


def _dot_tn(a, b):
    return lax.dot_general(a, b, (((0,), (0,)), ((), ())), preferred_element_type=F32)


def _resident(shape):
    return pl.BlockSpec(shape, lambda *_: (0,) * len(shape), pipeline_mode=pl.Buffered(1))


def _ffn_body(x_ref, g_ref, wg_ref, wu_ref, wd_ref, *rest, f_chunk, final_norm):
    o_ref = rest[-1]
    x = x_ref[...]
    xn = _rms(x, g_ref[...]).astype(BF16)
    ffn = None
    for j in range(wg_ref.shape[1] // f_chunk):
        cols = slice(j * f_chunk, (j + 1) * f_chunk)
        gate = _dot(xn, wg_ref[:, cols])
        up = _dot(xn, wu_ref[:, cols])
        act = (gate * jax.nn.sigmoid(gate) * up).astype(BF16)
        part = _dot(act, wd_ref[cols, :])
        ffn = part if ffn is None else ffn + part
    y = x + FFN_RES * ffn
    if final_norm:
        y = _rms(y, rest[0][...])
    o_ref[...] = y


def _ffn(h, g, wg, wu, wd, g_final=None, *, tm=512, f_chunk=1408):
    n, d = h.shape
    f = wg.shape[1]
    final_norm = g_final is not None
    row = pl.BlockSpec((tm, d), lambda i: (i, 0))
    in_specs = [row, _resident((1, d)), _resident((d, f)), _resident((d, f)), _resident((f, d))]
    args = [h, g, wg, wu, wd]
    if final_norm:
        in_specs.append(_resident((1, d)))
        args.append(g_final)
    return pl.pallas_call(
        functools.partial(_ffn_body, f_chunk=f_chunk, final_norm=final_norm),
        grid=(n // tm,),
        in_specs=in_specs,
        out_specs=row,
        out_shape=jax.ShapeDtypeStruct((n, d), F32),
        compiler_params=pltpu.CompilerParams(dimension_semantics=("arbitrary",), vmem_limit_bytes=VMEM_LIMIT),
        name="ffn",
    )(*args)


ATTN_SLABS = 5
_Q0, _Q1, _K, _V0, _V1 = range(ATTN_SLABS)


def _proj_body(x_ref, g_ref, w_ref, oh_ref, *rest):
    oa_refs, stage = rest[:-1], rest[-1]
    xn = _rms(x_ref[...], g_ref[...]).astype(BF16)
    n_h = oh_ref.shape[1]
    pa = _dot(xn, w_ref[:, n_h:])
    oh_ref[...] = _dot(xn, w_ref[:, :n_h])
    tm = pa.shape[0]
    slab_w = ATTN_SLABS * LANES
    lane = lax.broadcasted_iota(jnp.int32, (tm, LANES), 1)
    head0 = lane < ATTN_HEAD_DIM
    for hp in range(ATTN_WIDTH // LANES):
        q, k, v = (pa[:, p * ATTN_WIDTH + hp * LANES: p * ATTN_WIDTH + (hp + 1) * LANES] for p in range(3))
        q = q * (1.0 / math.sqrt(ATTN_HEAD_DIM))
        slabs = {
            _Q0: jnp.where(head0, q, 0.0),
            _Q1: jnp.where(head0, 0.0, q),
            _K: k,
            _V0: jnp.where(head0, v, 1.0),
            _V1: jnp.where(head0, 1.0, v),
        }
        for i, val in slabs.items():
            stage[0, hp * ATTN_SLABS + i] = val
            oa_refs[0][hp, :, i * LANES:(i + 1) * LANES] = val.astype(BF16)
    for lvl in range(1, len(ATTN_DILATIONS)):
        prev, dil = ATTN_DILATIONS[lvl - 1], ATTN_DILATIONS[lvl]
        step = dil // prev
        n_prev, n_rows = tm // prev, tm // dil
        for si in range(stage.shape[1]):
            hp, i = divmod(si, ATTN_SLABS)
            for r in range(dil):
                hi, lo = divmod(r, prev)
                x = stage[lvl - 1, si, pl.ds(lo * n_prev + hi, n_rows, stride=step), :]
                oa_refs[lvl][hp, :, r * slab_w + i * LANES: r * slab_w + (i + 1) * LANES] = x.astype(BF16)
                if lvl + 1 < len(ATTN_DILATIONS):
                    stage[lvl, si, r * n_rows:(r + 1) * n_rows, :] = x


def _proj(h, g, w_in, *, tm=512):
    b, s, d = h.shape
    n_h = HGRN_PARTS * HGRN_WIDTH
    pairs = ATTN_WIDTH // LANES
    slab_w = ATTN_SLABS * LANES
    return pl.pallas_call(
        _proj_body,
        grid=(b, s // tm),
        in_specs=[pl.BlockSpec((None, tm, d), lambda bi, i: (bi, i, 0)), _resident((1, d)), _resident(w_in.shape)],
        out_specs=[pl.BlockSpec((None, tm, n_h), lambda bi, i: (bi, i, 0))]
                  + [pl.BlockSpec((None, pairs, tm // dil, dil * slab_w), lambda bi, i: (bi, 0, i, 0))
                     for dil in ATTN_DILATIONS],
        out_shape=[jax.ShapeDtypeStruct((b, s, n_h), F32)]
                  + [jax.ShapeDtypeStruct((b, pairs, s // dil, dil * slab_w), BF16) for dil in ATTN_DILATIONS],
        scratch_shapes=[pltpu.VMEM((len(ATTN_DILATIONS) - 1, pairs * ATTN_SLABS, tm, LANES), F32)],
        compiler_params=pltpu.CompilerParams(dimension_semantics=("arbitrary", "arbitrary"),
                                             vmem_limit_bytes=VMEM_LIMIT),
        name="mix_proj",
    )(h, g, w_in)


_HGRN_LEVELS = tuple(2 ** e for e in range(int(math.log2(HGRN_CHUNK))))


def _hgrn_tables(c):
    t = np.arange(c)[:, None]
    s = np.arange(c)[None, :]
    cum = np.stack([(s <= t), (s >= t)]).astype(np.float32)
    cum = jnp.asarray(cum, BF16)
    masks = np.zeros((2, len(_HGRN_LEVELS) + 1, c, c), np.float32)
    masks[:, 0] = np.eye(c)
    for li, m in enumerate(_HGRN_LEVELS):
        same = (t // (2 * m)) == (s // (2 * m))
        t_hi, s_hi = (t % (2 * m)) >= m, (s % (2 * m)) >= m
        masks[0, li + 1] = same & t_hi & ~s_hi
        masks[1, li + 1] = same & ~t_hi & s_hi
    return jnp.asarray(cum), jnp.asarray(masks)


def _pivot_rows(x, m, fwd):
    c, w = x.shape
    r = m - 1 if fwd else m
    if 2 * m >= 8:
        xb = x.reshape(c // (2 * m), 2 * m, w)
        return jnp.broadcast_to(xb[:, r:r + 1, :], xb.shape).reshape(c, w)
    j = lax.broadcasted_iota(jnp.int32, x.shape, 0) % (2 * m)
    out = x
    for jj in range(2 * m):
        if jj != r:
            out = jnp.where(j == jj, pltpu.roll(x, (jj - r) % c, 0), out)
    return out


def _hgrn_gates(z, log_lb, log1m_lb, one_m_lb, one):
    log_sig = jnp.minimum(z, 0.0) - jnp.log(one + jnp.exp(-jnp.abs(z)))
    b = log1m_lb + log_sig
    log_f = jnp.maximum(log_lb, b) + jnp.log(one + jnp.exp(-jnp.abs(log_lb - b)))
    return log_f, one_m_lb * jnp.exp(log_sig - z)


def _cumsum_rows(cum_bf, x):
    w = x.shape[1]
    hi = x.astype(BF16)
    rest = x - hi.astype(F32)
    mid = rest.astype(BF16)
    lo = (rest - mid.astype(F32)).astype(BF16)
    y = _dot(cum_bf, jnp.concatenate([hi, mid, lo], axis=1))
    return y[:, :w] + y[:, w:2 * w] + y[:, 2 * w:]


def _hgrn_group(items, states, cum_ref, mask_ref):
    c = HGRN_CHUNK
    xs = [_cumsum_rows(cum_ref[d], log_f * LOG2_E) for _, _, _, log_f, d in items]
    q_bf = [q.astype(BF16) for q, _, _, _, _ in items]
    k_bf = [k.astype(BF16) for _, k, _, _, _ in items]
    v_bf = [v.astype(BF16) for _, _, v, _, _ in items]
    masks = [mask_ref[d, 0] for d in (0, 1)]
    scores = [_dot_nt(q, k) * masks[d] for q, k, (_, _, _, _, d) in zip(q_bf, k_bf, items)]
    for li, m in enumerate(_HGRN_LEVELS):
        masks = [mask_ref[d, li + 1] for d in (0, 1)]
        for j, (_, _, _, _, d) in enumerate(items):
            e = jnp.exp2(-jnp.abs(xs[j] - _pivot_rows(xs[j], m, d == 0))).astype(BF16)
            scores[j] = scores[j] + _dot_nt(q_bf[j] * e, k_bf[j] * e) * masks[d]
    outs = [_dot(sc.astype(BF16), v) for sc, v in zip(scores, v_bf)]
    edges = [x[c - 1:c, :] if d == 0 else x[0:1, :] for x, (_, _, _, _, d) in zip(xs, items)]
    handed = [_dot_tn(v, (k * jnp.exp2(edge - x)).astype(BF16))
              for v, edge, x, (_, k, _, _, _) in zip(v_bf, edges, xs, items)]
    q_in = [(q * jnp.exp2(x)).astype(BF16) for x, (q, _, _, _, _) in zip(xs, items)]
    states = list(states)
    for j, (_, _, _, _, d) in enumerate(items):
        outs[j] = outs[j] + _dot_nt(q_in[j], states[d].astype(BF16))
        states[d] = states[d] * jnp.exp2(edges[j]) + handed[j]
    return outs, states


def _hgrn_body(q_ref, i_ref, zf_ref, zb_ref, g_ref, c_ref, cum_ref, mask_ref, o_ref, acc_ref):
    s, dk = q_ref.shape
    c = HGRN_CHUNK
    n = s // c
    consts = c_ref[...]
    acc_ref[...] = jnp.zeros_like(acc_ref)
    z_refs = (zf_ref, zb_ref)

    def step(gi, states):
        rows, items = [], []
        for u in range(HGRN_UNROLL):
            for d in (0, 1):
                ci = gi * HGRN_UNROLL + u
                r = pl.ds(pl.multiple_of((ci if d == 0 else n - 1 - ci) * c, c), c)
                log_f, k = _hgrn_gates(z_refs[d][r, :], *(consts[3 * d + i:3 * d + i + 1] for i in range(3)),
                                       consts[7:8])
                rows.append(r)
                items.append((q_ref[r, :], k, i_ref[r, :], log_f, d))
        outs, states = _hgrn_group(items, states, cum_ref, mask_ref)
        for r, o in zip(rows, outs):
            acc_ref[r, :] += o
        return tuple(states)

    zero = jnp.zeros((dk, dk), F32)
    lax.fori_loop(0, n // HGRN_UNROLL, step, (zero, zero))

    o = acc_ref[...]
    o = o * lax.rsqrt(jnp.mean(o * o, axis=-1, keepdims=True) + EPS) * consts[6:7]
    g = g_ref[...]
    o_ref[...] = (o * (g * jax.nn.sigmoid(g))).astype(o_ref.dtype)


def _hgrn(p_h, consts, cum, masks):
    b, s, _ = p_h.shape
    dk = HGRN_HEAD_DIM

    def part(p):
        return pl.BlockSpec((None, s, dk), lambda bi, hi, p=p: (bi, 0, p * HGRN_HEADS + hi))

    return pl.pallas_call(
        _hgrn_body,
        grid=(b, HGRN_HEADS),
        in_specs=[part(0), part(1), part(2), part(3), part(4),
                  pl.BlockSpec((8, dk), lambda bi, hi: (0, hi)),
                  _resident(cum.shape), _resident(masks.shape)],
        out_specs=pl.BlockSpec((None, s, dk), lambda bi, hi: (bi, 0, hi)),
        out_shape=jax.ShapeDtypeStruct((b, s, HGRN_WIDTH), BF16),
        scratch_shapes=[pltpu.VMEM((s, dk), F32)],
        compiler_params=pltpu.CompilerParams(dimension_semantics=("arbitrary", "arbitrary"),
                                             vmem_limit_bytes=VMEM_LIMIT),
        name="hgrn2",
    )(p_h, p_h, p_h, p_h, p_h, consts, cum, masks)


def _attn_geometry(s_len, dil):
    length = s_len // dil
    nq = min(ATTN_QBLK, length)
    nk = min(nq + 2 * ATTN_HALF, length)
    return length, nq, nk, length // nq


def _attn_body(x1_ref, x4_ref, x16_ref, sl_ref, o_ref, bias_s, m_s, acc_s):
    x_refs = (x1_ref, x4_ref, x16_ref)
    s_len = x1_ref.shape[0]
    heads = LANES // ATTN_HEAD_DIM
    slab_w = ATTN_SLABS * LANES

    @pl.when(pl.program_id(1) == 0)
    def _():
        for di, dil in enumerate(ATTN_DILATIONS):
            _, nq, nk, _ = _attn_geometry(s_len, dil)
            rel0 = lax.broadcasted_iota(jnp.int32, (nq, nk), 1) - lax.broadcasted_iota(jnp.int32, (nq, nk), 0)
            for pos, offset in enumerate((0, -ATTN_HALF, nq - nk)):
                dist = jnp.abs(rel0 + offset).astype(F32)
                for h in range(heads):
                    pen = -(sl_ref[h:h + 1, :nk] * (float(dil) * dist))
                    bias_s[di, pos, h, :nq, :nk] = jnp.where(dist <= float(ATTN_HALF), pen, -jnp.inf)

    def slab(item, rows, i):
        di, _, _, r = item
        return x_refs[di][rows, r * slab_w + i * LANES: r * slab_w + (i + 1) * LANES]

    def rows_of(item):
        _, dil, bi, r = item
        length, nq, nk, n_blk = _attn_geometry(s_len, dil)
        q0 = bi * nq
        k0 = min(max(q0 - ATTN_HALF, 0), length - nk)
        pos = 0 if bi == 0 else (2 if bi == n_blk - 1 else 1)
        out_rows = pl.ds(r + dil * q0, nq, stride=dil) if dil > 1 else pl.ds(q0, nq)
        return pl.ds(q0, nq), pl.ds(k0, nk), out_rows, pos, nq, nk

    def scores(item):
        q_rows, k_rows, _, pos, nq, nk = rows_of(item)
        kb = slab(item, k_rows, _K)
        return [_dot_nt(slab(item, q_rows, (_Q0, _Q1)[h]), kb) + bias_s[item[0], pos, h, :nq, :nk]
                for h in range(heads)]

    def probs(s):
        m = jnp.max(s, axis=-1, keepdims=True)
        return m, jnp.exp(s - m).astype(BF16)

    def finish(item, h, m, p):
        _, k_rows, out_rows, _, nq, _ = rows_of(item)
        m_s[item[0], h, out_rows, :] = jnp.broadcast_to(m, (nq, LANES))
        acc_s[item[0], h, out_rows, :] = _dot(p, slab(item, k_rows, (_V0, _V1)[h]))

    items = [(di, dil, bi, r) for di, dil in enumerate(ATTN_DILATIONS)
             for bi in range(_attn_geometry(s_len, dil)[3]) for r in range(dil)]
    groups = [items[i:i + ATTN_BLOCKS_PER_BODY] for i in range(0, len(items), ATTN_BLOCKS_PER_BODY)]
    pending = None
    for group in groups + [None]:
        upcoming = [(item, scores(item)) for item in group] if group else None
        if pending:
            for item, ss in pending:
                for h, s in enumerate(ss):
                    finish(item, h, *probs(s))
        pending = upcoming

    rows_per_step = ATTN_QBLK
    head0 = lax.broadcasted_iota(jnp.int32, (rows_per_step, LANES), 1) < ATTN_HEAD_DIM

    def merge(ti, carry):
        rows = pl.ds(pl.multiple_of(ti * rows_per_step, rows_per_step), rows_per_step)
        acc = []
        for h in range(heads):
            ms = [m_s[di, h, rows, :] for di in range(len(ATTN_DILATIONS))]
            m_tot = functools.reduce(jnp.maximum, ms)
            acc.append(sum(jnp.exp(m - m_tot) * acc_s[di, h, rows, :] for di, m in enumerate(ms)))
        num = jnp.where(head0, acc[0], acc[1])
        den = pltpu.roll(jnp.where(head0, acc[1], acc[0]), ATTN_HEAD_DIM, 1)
        o_ref[rows, :] = (num / den).astype(o_ref.dtype)
        return carry

    lax.fori_loop(0, s_len // rows_per_step, merge, 0)


def _attn(p_views, slopes):
    b, pairs, s, slab_w = p_views[0].shape
    heads = LANES // ATTN_HEAD_DIM
    n_dil = len(ATTN_DILATIONS)
    nk_max = ATTN_QBLK + 2 * ATTN_HALF

    def view_spec(dil):
        return pl.BlockSpec((None, None, s // dil, dil * slab_w), lambda hi, bi: (bi, hi, 0, 0))

    return pl.pallas_call(
        _attn_body,
        grid=(pairs, b),
        in_specs=[view_spec(d) for d in ATTN_DILATIONS]
                 + [pl.BlockSpec((None, 8, nk_max), lambda hi, bi: (hi, 0, 0))],
        out_specs=pl.BlockSpec((None, s, LANES), lambda hi, bi: (bi, 0, hi)),
        out_shape=jax.ShapeDtypeStruct((b, s, ATTN_WIDTH), BF16),
        scratch_shapes=[pltpu.VMEM((n_dil, 3, heads, ATTN_QBLK, nk_max), F32),
                        pltpu.VMEM((n_dil, heads, s, LANES), F32),
                        pltpu.VMEM((n_dil, heads, s, LANES), F32)],
        compiler_params=pltpu.CompilerParams(dimension_semantics=("arbitrary", "arbitrary"),
                                             vmem_limit_bytes=VMEM_LIMIT),
        name="dilated_attn",
    )(*p_views, slopes)


def _memkv_body(m_ref, g_ref, w_ref, k_ref, v_ref):
    mn = _rms(m_ref[...], g_ref[...]).astype(BF16)
    d = k_ref.shape[1]
    k_ref[...] = _dot(mn, w_ref[:, :d]).astype(BF16)
    v_ref[...] = _dot(mn, w_ref[:, d:]).astype(BF16)


def _memkv(mem, g, w_kv):
    b, m, d = mem.shape
    blk = pl.BlockSpec((None, m, d), lambda bi: (bi, 0, 0))
    return pl.pallas_call(
        _memkv_body,
        grid=(b,),
        in_specs=[blk, _resident((1, d)), _resident(w_kv.shape)],
        out_specs=[blk, blk],
        out_shape=[jax.ShapeDtypeStruct((b, m, d), BF16)] * 2,
        compiler_params=pltpu.CompilerParams(dimension_semantics=("arbitrary",), vmem_limit_bytes=VMEM_LIMIT),
        name="mem_kv",
    )(mem, g, w_kv)


def _mix_xattn_body(h_ref, yh_ref, ya_ref, wo_h_ref, wo_a_ref, g_ref, wq_ref, k_ref, v_ref, wxo_ref, o_ref):
    h = h_ref[...] + _dot(yh_ref[...], wo_h_ref[...]) + _dot(ya_ref[...], wo_a_ref[...])
    d = h.shape[1]
    dh = d // MEM_HEADS
    hn = _rms(h, g_ref[...]).astype(BF16)
    q = (_dot(hn, wq_ref[...]) * (1.0 / math.sqrt(dh))).astype(BF16)
    outs = []
    for hd in range(MEM_HEADS):
        cols = slice(hd * dh, (hd + 1) * dh)
        s = _dot_nt(q[:, cols], k_ref[:, cols])
        p = jnp.exp(s - jnp.max(s, axis=-1, keepdims=True))
        l = jnp.sum(p, axis=-1, keepdims=True)
        outs.append((_dot(p.astype(BF16), v_ref[:, cols]) / l).astype(BF16))
    o_ref[...] = h + _dot(jnp.concatenate(outs, axis=1), wxo_ref[...])


def _mix_xattn(h, y_h, y_a, wo_h, wo_a, g, wq, k, v, wxo, *, tm=512):
    b, s, d = h.shape
    m = k.shape[1]

    def rows(w):
        return pl.BlockSpec((None, tm, w), lambda bi, i: (bi, i, 0))

    mem = pl.BlockSpec((None, m, d), lambda bi, i: (bi, 0, 0))
    return pl.pallas_call(
        _mix_xattn_body,
        grid=(b, s // tm),
        in_specs=[rows(d), rows(y_h.shape[2]), rows(y_a.shape[2]), _resident(wo_h.shape), _resident(wo_a.shape),
                  _resident((1, d)), _resident(wq.shape), mem, mem, _resident(wxo.shape)],
        out_specs=rows(d),
        out_shape=jax.ShapeDtypeStruct((b, s, d), F32),
        compiler_params=pltpu.CompilerParams(dimension_semantics=("arbitrary", "arbitrary"),
                                             vmem_limit_bytes=VMEM_LIMIT),
        name="mix_out_xattn",
    )(h, y_h, y_a, wo_h, wo_a, g, wq, k, v, wxo)


def _hgrn_consts(lb_fwd, lb_bwd, out_gain):
    rows = []
    for lb in (lb_fwd, lb_bwd):
        rows += [jnp.log(lb), jnp.log1p(-lb), 1.0 - lb]
    rows += [out_gain, jnp.ones_like(out_gain)]
    return jnp.stack(rows).astype(F32)


def _attn_slopes():
    sl = np.array([2.0 ** (-8.0 * (h + 1) / ATTN_HEADS) for h in range(ATTN_HEADS)], np.float32)
    per_pair = LANES // ATTN_HEAD_DIM
    out = np.zeros((ATTN_HEADS // per_pair, 8, ATTN_QBLK + 2 * ATTN_HALF), np.float32)
    for h in range(ATTN_HEADS):
        out[h // per_pair, h % per_pair, :] = sl[h]
    return jnp.asarray(out)


def kernel(x, mem, ln_ffn1, ffn1_w_gate, ffn1_w_up, ffn1_w_down, ln_mix, w_in, hgrn_lb_logits, hgrn_out_norm,
           w_out, ln_xq, ln_mem, w_xq, w_xkv, w_xo, ln_ffn2, ffn2_w_gate, ffn2_w_up, ffn2_w_down, ln_final):
    b, s, d = x.shape
    depth = ln_ffn1.shape[0]
    assert s % (ATTN_DILATIONS[-1] * 2 * ATTN_HALF) == 0 and s % ATTN_QBLK == 0 and s % HGRN_CHUNK == 0

    lb_all = jnp.cumsum(jax.nn.softmax(hgrn_lb_logits.astype(F32), axis=0), axis=0)
    lb_all = lb_all - lb_all[0:1]
    cum, masks = _hgrn_tables(HGRN_CHUNK)
    slopes = _attn_slopes()
    bf = lambda w: w.astype(BF16)
    vec = lambda g: g.astype(F32).reshape(1, -1)

    h = x.reshape(b * s, d)
    for l in range(depth):
        h = _ffn(h, vec(ln_ffn1[l]), bf(ffn1_w_gate[l]), bf(ffn1_w_up[l]), bf(ffn1_w_down[l]))
        p_h, *p_views = _proj(h.reshape(b, s, d), vec(ln_mix[l]), bf(w_in[l]))
        y_h = _hgrn(p_h, _hgrn_consts(lb_all[l, 0], lb_all[l, 1], hgrn_out_norm[l]), cum, masks)
        y_a = _attn(p_views, slopes)
        k_mem, v_mem = _memkv(mem, vec(ln_mem[l]), bf(w_xkv[l]))
        wo = bf(w_out[l])
        h = _mix_xattn(h.reshape(b, s, d), y_h, y_a, wo[:HGRN_WIDTH], wo[HGRN_WIDTH:], vec(ln_xq[l]),
                       bf(w_xq[l]), k_mem, v_mem, bf(w_xo[l])).reshape(b * s, d)
        last = l == depth - 1
        h = _ffn(h, vec(ln_ffn2[l]), bf(ffn2_w_gate[l]), bf(ffn2_w_up[l]), bf(ffn2_w_down[l]),
                 vec(ln_final) if last else None)
    return h.reshape(b, s, d)
```

```python
import functools
import math

import numpy as np
import jax
import jax.numpy as jnp
from jax import lax
from jax.experimental import pallas as pl
from jax.experimental.pallas import tpu as pltpu

F32 = jnp.float32
BF16 = jnp.bfloat16

EPS = 1e-6
FFN_RES = 0.5
LOG2_E = math.log2(math.e)

HGRN_HEAD_DIM = 128
HGRN_HEADS = 4
HGRN_WIDTH = HGRN_HEAD_DIM * HGRN_HEADS
HGRN_PARTS = 5
HGRN_CHUNK = 64
HGRN_UNROLL = 4
ATTN_HEAD_DIM = 64
ATTN_HEADS = 8
ATTN_WIDTH = ATTN_HEAD_DIM * ATTN_HEADS
ATTN_HALF = 64
ATTN_DILATIONS = (1, 4, 16)
ATTN_QBLK = 128
ATTN_BLOCKS_PER_BODY = 2
MEM_HEADS = 4

LANES = 128
VMEM_LIMIT = 56 * 1024 * 1024


def _rms(x, g):
    return x * lax.rsqrt(jnp.mean(x * x, axis=-1, keepdims=True) + EPS) * g


def _dot(a, b):
    return jnp.dot(a, b, preferred_element_type=F32)


def _dot_nt(a, b):
    return lax.dot_general(a, b, (((1,), (1,)), ((), ())), preferred_element_type=F32)


def _dot_tn(a, b):
    return lax.dot_general(a, b, (((0,), (0,)), ((), ())), preferred_element_type=F32)


def _resident(shape):
    return pl.BlockSpec(shape, lambda *_: (0,) * len(shape), pipeline_mode=pl.Buffered(1))


def _ffn_body(x_ref, g_ref, wg_ref, wu_ref, wd_ref, *rest, f_chunk, final_norm):
    o_ref = rest[-1]
    x = x_ref[...]
    xn = _rms(x, g_ref[...]).astype(BF16)
    ffn = None
    for j in range(wg_ref.shape[1] // f_chunk):
        cols = slice(j * f_chunk, (j + 1) * f_chunk)
        gate = _dot(xn, wg_ref[:, cols])
        up = _dot(xn, wu_ref[:, cols])
        act = (gate * jax.nn.sigmoid(gate) * up).astype(BF16)
        part = _dot(act, wd_ref[cols, :])
        ffn = part if ffn is None else ffn + part
    y = x + FFN_RES * ffn
    if final_norm:
        y = _rms(y, rest[0][...])
    o_ref[...] = y


def _ffn(h, g, wg, wu, wd, g_final=None, *, tm=512, f_chunk=2816):
    n, d = h.shape
    f = wg.shape[1]
    final_norm = g_final is not None
    row = pl.BlockSpec((tm, d), lambda i: (i, 0))
    in_specs = [row, _resident((1, d)), _resident((d, f)), _resident((d, f)), _resident((f, d))]
    args = [h, g, wg, wu, wd]
    if final_norm:
        in_specs.append(_resident((1, d)))
        args.append(g_final)
    return pl.pallas_call(
        functools.partial(_ffn_body, f_chunk=f_chunk, final_norm=final_norm),
        grid=(n // tm,),
        in_specs=in_specs,
        out_specs=row,
        out_shape=jax.ShapeDtypeStruct((n, d), F32),
        compiler_params=pltpu.CompilerParams(dimension_semantics=("arbitrary",), vmem_limit_bytes=VMEM_LIMIT),
        name="ffn",
    )(*args)


ATTN_SLABS = 5
_Q0, _Q1, _K, _V0, _V1 = range(ATTN_SLABS)


def _proj_body(x_ref, g_ref, w_ref, oh_ref, *rest):
    oa_refs, stage = rest[:-1], rest[-1]
    xn = _rms(x_ref[...], g_ref[...]).astype(BF16)
    n_h = oh_ref.shape[1]
    pa = _dot(xn, w_ref[:, n_h:])
    hgrn_parts = iter(range(HGRN_PARTS))

    def emit_hgrn_parts(count):
        for _ in range(count):
            p = next(hgrn_parts, None)
            if p is not None:
                cols = slice(p * HGRN_WIDTH, (p + 1) * HGRN_WIDTH)
                oh_ref[:, cols] = _dot(xn, w_ref[:, cols])

    tm = pa.shape[0]
    slab_w = ATTN_SLABS * LANES
    lane = lax.broadcasted_iota(jnp.int32, (tm, LANES), 1)
    head0 = lane < ATTN_HEAD_DIM
    for hp in range(ATTN_WIDTH // LANES):
        q, k, v = (pa[:, p * ATTN_WIDTH + hp * LANES: p * ATTN_WIDTH + (hp + 1) * LANES] for p in range(3))
        q = q * (1.0 / math.sqrt(ATTN_HEAD_DIM))
        slabs = {
            _Q0: jnp.where(head0, q, 0.0),
            _Q1: jnp.where(head0, 0.0, q),
            _K: k,
            _V0: jnp.where(head0, v, 1.0),
            _V1: jnp.where(head0, 1.0, v),
        }
        for i, val in slabs.items():
            stage[0, hp * ATTN_SLABS + i] = val
            oa_refs[0][hp, :, i * LANES:(i + 1) * LANES] = val.astype(BF16)
    for lvl in range(1, len(ATTN_DILATIONS)):
        prev, dil = ATTN_DILATIONS[lvl - 1], ATTN_DILATIONS[lvl]
        step = dil // prev
        n_prev, n_rows = tm // prev, tm // dil
        emit_hgrn_parts(2)
        for si in range(stage.shape[1]):
            hp, i = divmod(si, ATTN_SLABS)
            for r in range(dil):
                hi, lo = divmod(r, prev)
                x = stage[lvl - 1, si, pl.ds(lo * n_prev + hi, n_rows, stride=step), :]
                oa_refs[lvl][hp, :, r * slab_w + i * LANES: r * slab_w + (i + 1) * LANES] = x.astype(BF16)
                if lvl + 1 < len(ATTN_DILATIONS):
                    stage[lvl, si, r * n_rows:(r + 1) * n_rows, :] = x
    emit_hgrn_parts(HGRN_PARTS)


def _proj(h, g, w_in, *, tm=512):
    b, s, d = h.shape
    n_h = HGRN_PARTS * HGRN_WIDTH
    pairs = ATTN_WIDTH // LANES
    slab_w = ATTN_SLABS * LANES
    return pl.pallas_call(
        _proj_body,
        grid=(b, s // tm),
        in_specs=[pl.BlockSpec((None, tm, d), lambda bi, i: (bi, i, 0)), _resident((1, d)), _resident(w_in.shape)],
        out_specs=[pl.BlockSpec((None, tm, n_h), lambda bi, i: (bi, i, 0))]
                  + [pl.BlockSpec((None, pairs, tm // dil, dil * slab_w), lambda bi, i: (bi, 0, i, 0))
                     for dil in ATTN_DILATIONS],
        out_shape=[jax.ShapeDtypeStruct((b, s, n_h), F32)]
                  + [jax.ShapeDtypeStruct((b, pairs, s // dil, dil * slab_w), BF16) for dil in ATTN_DILATIONS],
        scratch_shapes=[pltpu.VMEM((len(ATTN_DILATIONS) - 1, pairs * ATTN_SLABS, tm, LANES), F32)],
        compiler_params=pltpu.CompilerParams(dimension_semantics=("arbitrary", "arbitrary"),
                                             vmem_limit_bytes=VMEM_LIMIT),
        name="mix_proj",
    )(h, g, w_in)


_HGRN_LEVELS = tuple(2 ** e for e in range(int(math.log2(HGRN_CHUNK))))


def _hgrn_tables(c):
    t = np.arange(c)[:, None]
    s = np.arange(c)[None, :]
    cum = np.stack([(s <= t), (s >= t)]).astype(np.float32)
    cum = jnp.asarray(cum, BF16)
    masks = np.zeros((2, len(_HGRN_LEVELS) + 1, c, c), np.float32)
    masks[:, 0] = np.eye(c)
    for li, m in enumerate(_HGRN_LEVELS):
        same = (t // (2 * m)) == (s // (2 * m))
        t_hi, s_hi = (t % (2 * m)) >= m, (s % (2 * m)) >= m
        masks[0, li + 1] = same & t_hi & ~s_hi
        masks[1, li + 1] = same & ~t_hi & s_hi
    return jnp.asarray(cum), jnp.asarray(masks)


def _level_exponent(x, lf, row, m, fwd):
    c, w = x.shape
    if m >= 4:
        r = m - 1 if fwd else m
        xb = x.reshape(c // (2 * m), 2 * m, w)
        d = x - jnp.broadcast_to(xb[:, r:r + 1, :], xb.shape).reshape(c, w)
        sign = jnp.uint32(0x80000000)
        return lax.bitcast_convert_type(lax.bitcast_convert_type(d, jnp.uint32) | sign, F32)
    j = row % (2 * m)
    after, before = pltpu.roll(lf, c - 1, 0), pltpu.roll(lf, 1, 0)
    if m == 1:
        return jnp.where(j == (1 if fwd else 0), lf, 0.0)
    assert m == 2
    if fwd:
        return jnp.where(j == 0, after, jnp.where(j == 1, 0.0, jnp.where(j == 2, lf, lf + before)))
    return jnp.where(j == 0, lf + after, jnp.where(j == 1, lf, jnp.where(j == 2, 0.0, before)))


def _hgrn_gates(z, log2_lb, log2_1m_lb, one_m_lb, one):
    z2 = z * LOG2_E
    log2_sig = jnp.minimum(z2, 0.0) - jnp.log2(one + jnp.exp2(-jnp.abs(z2)))
    b = log2_1m_lb + log2_sig
    log2_f = jnp.maximum(log2_lb, b) + jnp.log2(one + jnp.exp2(-jnp.abs(log2_lb - b)))
    return log2_f, one_m_lb * jnp.exp2(log2_sig - z2)


def _cumsum_rows(cum_bf, x):
    w = x.shape[1]
    hi = x.astype(BF16)
    rest = x - hi.astype(F32)
    mid = rest.astype(BF16)
    lo = (rest - mid.astype(F32)).astype(BF16)
    y = _dot(cum_bf, jnp.concatenate([hi, mid, lo], axis=1))
    return y[:, :w] + y[:, w:2 * w] + y[:, 2 * w:]


def _hgrn_group(items, states, cum_ref, mask_ref):
    c = HGRN_CHUNK
    xs = [_cumsum_rows(cum_ref[d], log2_f) for _, _, _, log2_f, d in items]
    row = lax.broadcasted_iota(jnp.int32, xs[0].shape, 0)
    q_bf = [q.astype(BF16) for q, _, _, _, _ in items]
    k_bf = [k.astype(BF16) for _, k, _, _, _ in items]
    v_bf = [v.astype(BF16) for _, _, v, _, _ in items]
    masks = [mask_ref[d, 0] for d in (0, 1)]
    scores = [_dot_nt(q, k) * masks[d] for q, k, (_, _, _, _, d) in zip(q_bf, k_bf, items)]
    for li, m in enumerate(_HGRN_LEVELS):
        masks = [mask_ref[d, li + 1] for d in (0, 1)]
        for j, (_, _, _, log2_f, d) in enumerate(items):
            e = jnp.exp2(_level_exponent(xs[j], log2_f, row, m, d == 0)).astype(BF16)
            scores[j] = scores[j] + _dot_nt(q_bf[j] * e, k_bf[j] * e) * masks[d]
    outs = [_dot(sc.astype(BF16), v) for sc, v in zip(scores, v_bf)]
    edges = [x[c - 1:c, :] if d == 0 else x[0:1, :] for x, (_, _, _, _, d) in zip(xs, items)]
    handed = [_dot_tn(v, (k * jnp.exp2(edge - x)).astype(BF16))
              for v, edge, x, (_, k, _, _, _) in zip(v_bf, edges, xs, items)]
    q_in = [(q * jnp.exp2(x)).astype(BF16) for x, (q, _, _, _, _) in zip(xs, items)]
    states = list(states)
    for j, (_, _, _, _, d) in enumerate(items):
        outs[j] = outs[j] + _dot_nt(q_in[j], states[d].astype(BF16))
        states[d] = states[d] * jnp.exp2(edges[j]) + handed[j]
    return outs, states


def _hgrn_body(q_ref, i_ref, zf_ref, zb_ref, g_ref, c_ref, cum_ref, mask_ref, o_ref, acc_ref):
    s, dk = q_ref.shape
    c = HGRN_CHUNK
    n = s // c
    consts = c_ref[...]
    acc_ref[...] = jnp.zeros_like(acc_ref)
    z_refs = (zf_ref, zb_ref)

    def step(gi, states):
        rows, items = [], []
        for u in range(HGRN_UNROLL):
            for d in (0, 1):
                ci = gi * HGRN_UNROLL + u
                r = pl.ds(pl.multiple_of((ci if d == 0 else n - 1 - ci) * c, c), c)
                log2_f, k = _hgrn_gates(z_refs[d][r, :], *(consts[3 * d + i:3 * d + i + 1] for i in range(3)),
                                        consts[7:8])
                rows.append(r)
                items.append((q_ref[r, :], k, i_ref[r, :], log2_f, d))
        outs, states = _hgrn_group(items, states, cum_ref, mask_ref)
        for r, o in zip(rows, outs):
            acc_ref[r, :] += o
        return tuple(states)

    zero = jnp.zeros((dk, dk), F32)
    lax.fori_loop(0, n // HGRN_UNROLL, step, (zero, zero))

    o = acc_ref[...]
    o = o * lax.rsqrt(jnp.mean(o * o, axis=-1, keepdims=True) + EPS) * consts[6:7]
    g = g_ref[...]
    o_ref[...] = (o * (g * jax.nn.sigmoid(g))).astype(o_ref.dtype)


def _hgrn(p_h, consts, cum, masks):
    b, s, _ = p_h.shape
    dk = HGRN_HEAD_DIM

    def part(p):
        return pl.BlockSpec((None, s, dk), lambda bi, hi, p=p: (bi, 0, p * HGRN_HEADS + hi))

    return pl.pallas_call(
        _hgrn_body,
        grid=(b, HGRN_HEADS),
        in_specs=[part(0), part(1), part(2), part(3), part(4),
                  pl.BlockSpec((8, dk), lambda bi, hi: (0, hi)),
                  _resident(cum.shape), _resident(masks.shape)],
        out_specs=pl.BlockSpec((None, s, dk), lambda bi, hi: (bi, 0, hi)),
        out_shape=jax.ShapeDtypeStruct((b, s, HGRN_WIDTH), BF16),
        scratch_shapes=[pltpu.VMEM((s, dk), F32)],
        compiler_params=pltpu.CompilerParams(dimension_semantics=("arbitrary", "arbitrary"),
                                             vmem_limit_bytes=VMEM_LIMIT),
        name="hgrn2",
    )(p_h, p_h, p_h, p_h, p_h, consts, cum, masks)


def _attn_geometry(s_len, dil):
    length = s_len // dil
    nq = min(ATTN_QBLK, length)
    nk = min(nq + 2 * ATTN_HALF, length)
    return length, nq, nk, length // nq


def _attn_body(x1_ref, x4_ref, x16_ref, sl_ref, o_ref, bias_s, m_s, num_s, den_s):
    x_refs = (x1_ref, x4_ref, x16_ref)
    s_len = x1_ref.shape[0]
    heads = LANES // ATTN_HEAD_DIM
    slab_w = ATTN_SLABS * LANES

    @pl.when(pl.program_id(1) == 0)
    def _():
        for di, dil in enumerate(ATTN_DILATIONS):
            _, nq, nk, _ = _attn_geometry(s_len, dil)
            rel0 = lax.broadcasted_iota(jnp.int32, (nq, nk), 1) - lax.broadcasted_iota(jnp.int32, (nq, nk), 0)
            for pos, offset in enumerate((0, -ATTN_HALF, nq - nk)):
                dist = jnp.abs(rel0 + offset).astype(F32)
                for h in range(heads):
                    pen = -(sl_ref[h:h + 1, :nk] * (float(dil) * dist))
                    bias_s[di, pos, h, :nq, :nk] = jnp.where(dist <= float(ATTN_HALF), pen, -jnp.inf)

    def slab(item, rows, i):
        di, _, _, r = item
        return x_refs[di][rows, r * slab_w + i * LANES: r * slab_w + (i + 1) * LANES]

    def rows_of(item):
        _, dil, bi, r = item
        length, nq, nk, n_blk = _attn_geometry(s_len, dil)
        q0 = bi * nq
        k0 = min(max(q0 - ATTN_HALF, 0), length - nk)
        pos = 0 if bi == 0 else (2 if bi == n_blk - 1 else 1)
        out_rows = pl.ds(r + dil * q0, nq, stride=dil) if dil > 1 else pl.ds(q0, nq)
        return pl.ds(q0, nq), pl.ds(k0, nk), out_rows, pos, nq, nk

    def scores(item):
        q_rows, k_rows, _, pos, nq, nk = rows_of(item)
        kb = slab(item, k_rows, _K)
        return [_dot_nt(slab(item, q_rows, (_Q0, _Q1)[h]), kb) + bias_s[item[0], pos, h, :nq, :nk]
                for h in range(heads)]

    def probs(s):
        m = jnp.max(s, axis=-1, keepdims=True)
        return m, jnp.exp(s - m).astype(BF16)

    def finish(item, stats):
        _, k_rows, out_rows, _, nq, _ = rows_of(item)
        head0 = lax.broadcasted_iota(jnp.int32, (nq, LANES), 1) < ATTN_HEAD_DIM
        acc = [_dot(p, slab(item, k_rows, (_V0, _V1)[h])) for h, (_, p) in enumerate(stats)]
        m_s[item[0], out_rows, :] = jnp.where(head0, stats[0][0], stats[1][0])
        num_s[item[0], out_rows, :] = jnp.where(head0, acc[0], acc[1])
        den_s[item[0], out_rows, :] = pltpu.roll(jnp.where(head0, acc[1], acc[0]), ATTN_HEAD_DIM, 1)

    items = [(di, dil, bi, r) for di, dil in enumerate(ATTN_DILATIONS)
             for bi in range(_attn_geometry(s_len, dil)[3]) for r in range(dil)]
    groups = [items[i:i + ATTN_BLOCKS_PER_BODY] for i in range(0, len(items), ATTN_BLOCKS_PER_BODY)]
    pending = None
    for group in groups + [None]:
        upcoming = [(item, scores(item)) for item in group] if group else None
        if pending:
            for item, ss in pending:
                finish(item, [probs(s) for s in ss])
        pending = upcoming

    rows_per_step = ATTN_QBLK

    def merge(ti, carry):
        rows = pl.ds(pl.multiple_of(ti * rows_per_step, rows_per_step), rows_per_step)
        ms = [m_s[di, rows, :] for di in range(len(ATTN_DILATIONS))]
        m_tot = functools.reduce(jnp.maximum, ms)
        ws = [jnp.exp(m - m_tot) for m in ms]
        num = sum(w * num_s[di, rows, :] for di, w in enumerate(ws))
        den = sum(w * den_s[di, rows, :] for di, w in enumerate(ws))
        o_ref[rows, :] = (num / den).astype(o_ref.dtype)
        return carry

    lax.fori_loop(0, s_len // rows_per_step, merge, 0)


def _attn(p_views, slopes):
    b, pairs, s, slab_w = p_views[0].shape
    heads = LANES // ATTN_HEAD_DIM
    n_dil = len(ATTN_DILATIONS)
    nk_max = ATTN_QBLK + 2 * ATTN_HALF

    def view_spec(dil):
        return pl.BlockSpec((None, None, s // dil, dil * slab_w), lambda hi, bi: (bi, hi, 0, 0))

    return pl.pallas_call(
        _attn_body,
        grid=(pairs, b),
        in_specs=[view_spec(d) for d in ATTN_DILATIONS]
                 + [pl.BlockSpec((None, 8, nk_max), lambda hi, bi: (hi, 0, 0))],
        out_specs=pl.BlockSpec((None, s, LANES), lambda hi, bi: (bi, 0, hi)),
        out_shape=jax.ShapeDtypeStruct((b, s, ATTN_WIDTH), BF16),
        scratch_shapes=[pltpu.VMEM((n_dil, 3, heads, ATTN_QBLK, nk_max), F32),
                        pltpu.VMEM((n_dil, s, LANES), F32),
                        pltpu.VMEM((n_dil, s, LANES), F32),
                        pltpu.VMEM((n_dil, s, LANES), F32)],
        compiler_params=pltpu.CompilerParams(dimension_semantics=("arbitrary", "arbitrary"),
                                             vmem_limit_bytes=VMEM_LIMIT),
        name="dilated_attn",
    )(*p_views, slopes)


def _memkv_body(m_ref, g_ref, w_ref, k_ref, v_ref):
    mn = _rms(m_ref[...], g_ref[...]).astype(BF16)
    d = k_ref.shape[1]
    k_ref[...] = _dot(mn, w_ref[:, :d]).astype(BF16)
    v_ref[...] = _dot(mn, w_ref[:, d:]).astype(BF16)


def _memkv(mem, g, w_kv):
    b, m, d = mem.shape
    blk = pl.BlockSpec((None, m, d), lambda bi: (bi, 0, 0))
    return pl.pallas_call(
        _memkv_body,
        grid=(b,),
        in_specs=[blk, _resident((1, d)), _resident(w_kv.shape)],
        out_specs=[blk, blk],
        out_shape=[jax.ShapeDtypeStruct((b, m, d), BF16)] * 2,
        compiler_params=pltpu.CompilerParams(dimension_semantics=("arbitrary",), vmem_limit_bytes=VMEM_LIMIT),
        name="mem_kv",
    )(mem, g, w_kv)


def _mix_xattn_body(h_ref, yh_ref, ya_ref, wo_h_ref, wo_a_ref, g_ref, wq_ref, k_ref, v_ref, wxo_ref, o_ref):
    h = h_ref[...] + _dot(yh_ref[...], wo_h_ref[...]) + _dot(ya_ref[...], wo_a_ref[...])
    d = h.shape[1]
    dh = d // MEM_HEADS
    hn = _rms(h, g_ref[...]).astype(BF16)
    q = (_dot(hn, wq_ref[...]) * (1.0 / math.sqrt(dh))).astype(BF16)
    outs = []
    for hd in range(MEM_HEADS):
        cols = slice(hd * dh, (hd + 1) * dh)
        s = _dot_nt(q[:, cols], k_ref[:, cols])
        p = jnp.exp(s - jnp.max(s, axis=-1, keepdims=True))
        l = jnp.sum(p, axis=-1, keepdims=True)
        outs.append((_dot(p.astype(BF16), v_ref[:, cols]) / l).astype(BF16))
    o_ref[...] = h + _dot(jnp.concatenate(outs, axis=1), wxo_ref[...])


def _mix_xattn(h, y_h, y_a, wo_h, wo_a, g, wq, k, v, wxo, *, tm=512):
    b, s, d = h.shape
    m = k.shape[1]

    def rows(w):
        return pl.BlockSpec((None, tm, w), lambda bi, i: (bi, i, 0))

    mem = pl.BlockSpec((None, m, d), lambda bi, i: (bi, 0, 0))
    return pl.pallas_call(
        _mix_xattn_body,
        grid=(b, s // tm),
        in_specs=[rows(d), rows(y_h.shape[2]), rows(y_a.shape[2]), _resident(wo_h.shape), _resident(wo_a.shape),
                  _resident((1, d)), _resident(wq.shape), mem, mem, _resident(wxo.shape)],
        out_specs=rows(d),
        out_shape=jax.ShapeDtypeStruct((b, s, d), F32),
        compiler_params=pltpu.CompilerParams(dimension_semantics=("arbitrary", "arbitrary"),
                                             vmem_limit_bytes=VMEM_LIMIT),
        name="mix_out_xattn",
    )(h, y_h, y_a, wo_h, wo_a, g, wq, k, v, wxo)


def _hgrn_consts(lb_fwd, lb_bwd, out_gain):
    rows = []
    for lb in (lb_fwd, lb_bwd):
        rows += [jnp.log(lb) * LOG2_E, jnp.log1p(-lb) * LOG2_E, 1.0 - lb]
    rows += [out_gain, jnp.ones_like(out_gain)]
    return jnp.stack(rows).astype(F32)


def _attn_slopes():
    sl = np.array([2.0 ** (-8.0 * (h + 1) / ATTN_HEADS) for h in range(ATTN_HEADS)], np.float32)
    per_pair = LANES // ATTN_HEAD_DIM
    out = np.zeros((ATTN_HEADS // per_pair, 8, ATTN_QBLK + 2 * ATTN_HALF), np.float32)
    for h in range(ATTN_HEADS):
        out[h // per_pair, h % per_pair, :] = sl[h]
    return jnp.asarray(out)


def kernel(x, mem, ln_ffn1, ffn1_w_gate, ffn1_w_up, ffn1_w_down, ln_mix, w_in, hgrn_lb_logits, hgrn_out_norm,
           w_out, ln_xq, ln_mem, w_xq, w_xkv, w_xo, ln_ffn2, ffn2_w_gate, ffn2_w_up, ffn2_w_down, ln_final):
    b, s, d = x.shape
    depth = ln_ffn1.shape[0]
    assert s % (ATTN_DILATIONS[-1] * 2 * ATTN_HALF) == 0 and s % ATTN_QBLK == 0 and s % HGRN_CHUNK == 0

    lb_all = jnp.cumsum(jax.nn.softmax(hgrn_lb_logits.astype(F32), axis=0), axis=0)
    lb_all = lb_all - lb_all[0:1]
    cum, masks = _hgrn_tables(HGRN_CHUNK)
    slopes = _attn_slopes()
    bf = lambda w: w.astype(BF16)
    vec = lambda g: g.astype(F32).reshape(1, -1)

    h = x.reshape(b * s, d)
    for l in range(depth):
        h = _ffn(h, vec(ln_ffn1[l]), bf(ffn1_w_gate[l]), bf(ffn1_w_up[l]), bf(ffn1_w_down[l]))
        p_h, *p_views = _proj(h.reshape(b, s, d), vec(ln_mix[l]), bf(w_in[l]))
        y_h = _hgrn(p_h, _hgrn_consts(lb_all[l, 0], lb_all[l, 1], hgrn_out_norm[l]), cum, masks)
        y_a = _attn(p_views, slopes)
        k_mem, v_mem = _memkv(mem, vec(ln_mem[l]), bf(w_xkv[l]))
        wo = bf(w_out[l])
        h = _mix_xattn(h.reshape(b, s, d), y_h, y_a, wo[:HGRN_WIDTH], wo[HGRN_WIDTH:], vec(ln_xq[l]),
                       bf(w_xq[l]), k_mem, v_mem, bf(w_xo[l])).reshape(b * s, d)
        last = l == depth - 1
        h = _ffn(h, vec(ln_ffn2[l]), bf(ffn2_w_gate[l]), bf(ffn2_w_up[l]), bf(ffn2_w_down[l]),
                 vec(ln_final) if last else None)
    return h.reshape(b, s, d)
```

```python
import functools
import math

import numpy as np
import jax
import jax.numpy as jnp
from jax import lax
from jax.experimental import pallas as pl
from jax.experimental.pallas import tpu as pltpu

F32 = jnp.float32
BF16 = jnp.bfloat16

EPS = 1e-6
FFN_RES = 0.5
LOG2_E = math.log2(math.e)

HGRN_HEAD_DIM = 128
HGRN_HEADS = 4
HGRN_WIDTH = HGRN_HEAD_DIM * HGRN_HEADS
HGRN_PARTS = 5
HGRN_CHUNK = 64
HGRN_UNROLL = 4
ATTN_HEAD_DIM = 64
ATTN_HEADS = 8
ATTN_WIDTH = ATTN_HEAD_DIM * ATTN_HEADS
ATTN_HALF = 64
ATTN_DILATIONS = (1, 4, 16)
ATTN_QBLK = 128
ATTN_BLOCKS_PER_BODY = 2
MEM_HEADS = 4

LANES = 128
VMEM_LIMIT = 56 * 1024 * 1024


def _rms(x, g):
    return x * lax.rsqrt(jnp.mean(x * x, axis=-1, keepdims=True) + EPS) * g


def _dot(a, b):
    return jnp.dot(a, b, preferred_element_type=F32)


def _dot_nt(a, b):
    return lax.dot_general(a, b, (((1,), (1,)), ((), ())), preferred_element_type=F32)


def _dot_tn(a, b):
    return lax.dot_general(a, b, (((0,), (0,)), ((), ())), preferred_element_type=F32)


def _resident(shape):
    return pl.BlockSpec(shape, lambda *_: (0,) * len(shape), pipeline_mode=pl.Buffered(1))


def _ffn_body(x_ref, g_ref, wg_ref, wu_ref, wd_ref, *rest, f_chunk, final_norm):
    o_ref = rest[-1]
    x = x_ref[...]
    xn = _rms(x, g_ref[...]).astype(BF16)
    ffn = None
    for j in range(wg_ref.shape[1] // f_chunk):
        cols = slice(j * f_chunk, (j + 1) * f_chunk)
        gate = _dot(xn, wg_ref[:, cols])
        up = _dot(xn, wu_ref[:, cols])
        act = (gate * jax.nn.sigmoid(gate) * up).astype(BF16)
        part = _dot(act, wd_ref[cols, :])
        ffn = part if ffn is None else ffn + part
    y = x + FFN_RES * ffn
    if final_norm:
        y = _rms(y, rest[0][...])
    o_ref[...] = y


def _ffn(h, g, wg, wu, wd, g_final=None, *, tm=512, f_chunk=2816):
    n, d = h.shape
    f = wg.shape[1]
    final_norm = g_final is not None
    row = pl.BlockSpec((tm, d), lambda i: (i, 0))
    in_specs = [row, _resident((1, d)), _resident((d, f)), _resident((d, f)), _resident((f, d))]
    args = [h, g, wg, wu, wd]
    if final_norm:
        in_specs.append(_resident((1, d)))
        args.append(g_final)
    return pl.pallas_call(
        functools.partial(_ffn_body, f_chunk=f_chunk, final_norm=final_norm),
        grid=(n // tm,),
        in_specs=in_specs,
        out_specs=row,
        out_shape=jax.ShapeDtypeStruct((n, d), F32),
        compiler_params=pltpu.CompilerParams(dimension_semantics=("arbitrary",), vmem_limit_bytes=VMEM_LIMIT),
        name="ffn",
    )(*args)


ATTN_SLABS = 5
_Q0, _Q1, _K, _V0, _V1 = range(ATTN_SLABS)


def _proj_body(x_ref, g_ref, w_ref, oh_ref, *rest):
    oa_refs, stage = rest[:-1], rest[-1]
    xn = _rms(x_ref[...], g_ref[...]).astype(BF16)
    n_h = oh_ref.shape[1]
    pa = _dot(xn, w_ref[:, n_h:])
    hgrn_parts = iter(range(HGRN_PARTS))

    def emit_hgrn_parts(count):
        for _ in range(count):
            p = next(hgrn_parts, None)
            if p is not None:
                cols = slice(p * HGRN_WIDTH, (p + 1) * HGRN_WIDTH)
                oh_ref[:, cols] = _dot(xn, w_ref[:, cols])

    tm = pa.shape[0]
    slab_w = ATTN_SLABS * LANES
    lane = lax.broadcasted_iota(jnp.int32, (tm, LANES), 1)
    head0 = lane < ATTN_HEAD_DIM
    for hp in range(ATTN_WIDTH // LANES):
        q, k, v = (pa[:, p * ATTN_WIDTH + hp * LANES: p * ATTN_WIDTH + (hp + 1) * LANES] for p in range(3))
        q = q * (1.0 / math.sqrt(ATTN_HEAD_DIM))
        slabs = {
            _Q0: jnp.where(head0, q, 0.0),
            _Q1: jnp.where(head0, 0.0, q),
            _K: k,
            _V0: jnp.where(head0, v, 1.0),
            _V1: jnp.where(head0, 1.0, v),
        }
        for i, val in slabs.items():
            stage[0, hp * ATTN_SLABS + i] = val
            oa_refs[0][hp, :, i * LANES:(i + 1) * LANES] = val.astype(BF16)
    for lvl in range(1, len(ATTN_DILATIONS)):
        prev, dil = ATTN_DILATIONS[lvl - 1], ATTN_DILATIONS[lvl]
        step = dil // prev
        n_prev, n_rows = tm // prev, tm // dil
        emit_hgrn_parts(2)
        for si in range(stage.shape[1]):
            hp, i = divmod(si, ATTN_SLABS)
            for r in range(dil):
                hi, lo = divmod(r, prev)
                x = stage[lvl - 1, si, pl.ds(lo * n_prev + hi, n_rows, stride=step), :]
                oa_refs[lvl][hp, :, r * slab_w + i * LANES: r * slab_w + (i + 1) * LANES] = x.astype(BF16)
                if lvl + 1 < len(ATTN_DILATIONS):
                    stage[lvl, si, r * n_rows:(r + 1) * n_rows, :] = x
    emit_hgrn_parts(HGRN_PARTS)


def _proj(h, g, w_in, *, tm=512):
    b, s, d = h.shape
    n_h = HGRN_PARTS * HGRN_WIDTH
    pairs = ATTN_WIDTH // LANES
    slab_w = ATTN_SLABS * LANES
    return pl.pallas_call(
        _proj_body,
        grid=(b, s // tm),
        in_specs=[pl.BlockSpec((None, tm, d), lambda bi, i: (bi, i, 0)), _resident((1, d)), _resident(w_in.shape)],
        out_specs=[pl.BlockSpec((None, tm, n_h), lambda bi, i: (bi, i, 0))]
                  + [pl.BlockSpec((None, pairs, tm // dil, dil * slab_w), lambda bi, i: (bi, 0, i, 0))
                     for dil in ATTN_DILATIONS],
        out_shape=[jax.ShapeDtypeStruct((b, s, n_h), F32)]
                  + [jax.ShapeDtypeStruct((b, pairs, s // dil, dil * slab_w), BF16) for dil in ATTN_DILATIONS],
        scratch_shapes=[pltpu.VMEM((len(ATTN_DILATIONS) - 1, pairs * ATTN_SLABS, tm, LANES), F32)],
        compiler_params=pltpu.CompilerParams(dimension_semantics=("arbitrary", "arbitrary"),
                                             vmem_limit_bytes=VMEM_LIMIT),
        name="mix_proj",
    )(h, g, w_in)


_HGRN_LEVELS = tuple(2 ** e for e in range(int(math.log2(HGRN_CHUNK))))


def _hgrn_tables(c):
    t = np.arange(c)[:, None]
    s = np.arange(c)[None, :]
    cum = np.stack([(s <= t), (s >= t)]).astype(np.float32)
    cum = jnp.asarray(cum, BF16)
    masks = np.zeros((2, len(_HGRN_LEVELS) + 1, c, c), np.float32)
    masks[:, 0] = np.eye(c)
    for li, m in enumerate(_HGRN_LEVELS):
        same = (t // (2 * m)) == (s // (2 * m))
        t_hi, s_hi = (t % (2 * m)) >= m, (s % (2 * m)) >= m
        masks[0, li + 1] = same & t_hi & ~s_hi
        masks[1, li + 1] = same & ~t_hi & s_hi
    return jnp.asarray(cum), jnp.asarray(masks, BF16)


def _neg_abs(x):
    bits = lax.bitcast_convert_type(x, jnp.uint32) | jnp.uint32(0x80000000)
    return lax.bitcast_convert_type(bits, F32)


def _level_exponent(x, lf, row, m, fwd):
    c, w = x.shape
    if m >= 4:
        r = m - 1 if fwd else m
        xb = x.reshape(c // (2 * m), 2 * m, w)
        return _neg_abs(x - jnp.broadcast_to(xb[:, r:r + 1, :], xb.shape).reshape(c, w))
    j = row % (2 * m)
    after, before = pltpu.roll(lf, c - 1, 0), pltpu.roll(lf, 1, 0)
    if m == 1:
        return jnp.where(j == (1 if fwd else 0), lf, 0.0)
    assert m == 2
    if fwd:
        return jnp.where(j == 0, after, jnp.where(j == 1, 0.0, jnp.where(j == 2, lf, lf + before)))
    return jnp.where(j == 0, lf + after, jnp.where(j == 1, lf, jnp.where(j == 2, 0.0, before)))


def _hgrn_gates(z, log2_lb, log2_1m_lb, one_m_lb, one):
    z2 = z * LOG2_E
    log2_sig = jnp.minimum(z2, 0.0) - jnp.log2(one + jnp.exp2(_neg_abs(z2)))
    b = log2_1m_lb + log2_sig
    log2_f = jnp.maximum(log2_lb, b) + jnp.log2(one + jnp.exp2(_neg_abs(log2_lb - b)))
    return log2_f, one_m_lb * jnp.exp2(log2_sig - z2)


def _cumsum_rows(cum_bf, x):
    w = x.shape[1]
    hi = x.astype(BF16)
    rest = x - hi.astype(F32)
    mid = rest.astype(BF16)
    lo = (rest - mid.astype(F32)).astype(BF16)
    y = _dot(cum_bf, jnp.concatenate([hi, mid, lo], axis=1))
    return y[:, :w] + y[:, w:2 * w] + y[:, 2 * w:]


def _hgrn_group(items, states, cum_ref, mask_ref):
    c = HGRN_CHUNK
    xs = [_cumsum_rows(cum_ref[d], log2_f) for _, _, _, log2_f, d in items]
    row = lax.broadcasted_iota(jnp.int32, xs[0].shape, 0)
    q_bf = [q.astype(BF16) for q, _, _, _, _ in items]
    k_bf = [k.astype(BF16) for _, k, _, _, _ in items]
    v_bf = [v.astype(BF16) for _, _, v, _, _ in items]
    masks = [mask_ref[d, 0] for d in (0, 1)]
    scores = [_dot_nt(q, k).astype(BF16) * masks[d] for q, k, (_, _, _, _, d) in zip(q_bf, k_bf, items)]
    for li, m in enumerate(_HGRN_LEVELS):
        masks = [mask_ref[d, li + 1] for d in (0, 1)]
        for j, (_, _, _, log2_f, d) in enumerate(items):
            e = jnp.exp2(_level_exponent(xs[j], log2_f, row, m, d == 0)).astype(BF16)
            scores[j] = scores[j] + _dot_nt(q_bf[j] * e, k_bf[j] * e).astype(BF16) * masks[d]
    outs = [_dot(sc, v) for sc, v in zip(scores, v_bf)]
    edges = [x[c - 1:c, :] if d == 0 else x[0:1, :] for x, (_, _, _, _, d) in zip(xs, items)]
    handed = [_dot_tn(v, (k * jnp.exp2(edge - x)).astype(BF16))
              for v, edge, x, (_, k, _, _, _) in zip(v_bf, edges, xs, items)]
    q_in = [(q * jnp.exp2(x)).astype(BF16) for x, (q, _, _, _, _) in zip(xs, items)]
    states = list(states)
    for j, (_, _, _, _, d) in enumerate(items):
        outs[j] = outs[j] + _dot_nt(q_in[j], states[d].astype(BF16))
        states[d] = states[d] * jnp.exp2(edges[j]) + handed[j]
    return outs, states


def _hgrn_body(q_ref, i_ref, zf_ref, zb_ref, g_ref, c_ref, cum_ref, mask_ref, o_ref, acc_ref):
    s, dk = q_ref.shape
    c = HGRN_CHUNK
    n = s // c
    consts = c_ref[...]
    acc_ref[...] = jnp.zeros_like(acc_ref)
    z_refs = (zf_ref, zb_ref)

    def step(gi, states):
        rows, items = [], []
        for u in range(HGRN_UNROLL):
            for d in (0, 1):
                ci = gi * HGRN_UNROLL + u
                r = pl.ds(pl.multiple_of((ci if d == 0 else n - 1 - ci) * c, c), c)
                log2_f, k = _hgrn_gates(z_refs[d][r, :], *(consts[3 * d + i:3 * d + i + 1] for i in range(3)),
                                        consts[7:8])
                rows.append(r)
                items.append((q_ref[r, :], k, i_ref[r, :], log2_f, d))
        outs, states = _hgrn_group(items, states, cum_ref, mask_ref)
        for r, o in zip(rows, outs):
            acc_ref[r, :] += o
        return tuple(states)

    zero = jnp.zeros((dk, dk), F32)
    lax.fori_loop(0, n // HGRN_UNROLL, step, (zero, zero), unroll=True)

    o = acc_ref[...]
    o = o * lax.rsqrt(jnp.mean(o * o, axis=-1, keepdims=True) + EPS) * consts[6:7]
    g = g_ref[...]
    o_ref[...] = (o * (g * jax.nn.sigmoid(g))).astype(o_ref.dtype)


def _hgrn(p_h, consts, cum, masks):
    b, s, _ = p_h.shape
    dk = HGRN_HEAD_DIM

    def part(p):
        return pl.BlockSpec((None, s, dk), lambda bi, hi, p=p: (bi, 0, p * HGRN_HEADS + hi))

    return pl.pallas_call(
        _hgrn_body,
        grid=(b, HGRN_HEADS),
        in_specs=[part(0), part(1), part(2), part(3), part(4),
                  pl.BlockSpec((8, dk), lambda bi, hi: (0, hi)),
                  _resident(cum.shape), _resident(masks.shape)],
        out_specs=pl.BlockSpec((None, s, dk), lambda bi, hi: (bi, 0, hi)),
        out_shape=jax.ShapeDtypeStruct((b, s, HGRN_WIDTH), BF16),
        scratch_shapes=[pltpu.VMEM((s, dk), F32)],
        compiler_params=pltpu.CompilerParams(dimension_semantics=("arbitrary", "arbitrary"),
                                             vmem_limit_bytes=VMEM_LIMIT),
        name="hgrn2",
    )(p_h, p_h, p_h, p_h, p_h, consts, cum, masks)


def _attn_geometry(s_len, dil):
    length = s_len // dil
    nq = min(ATTN_QBLK, length)
    nk = min(nq + 2 * ATTN_HALF, length)
    return length, nq, nk, length // nq


def _attn_body(x1_ref, x4_ref, x16_ref, sl_ref, o_ref, bias_s, m_s, num_s, den_s):
    x_refs = (x1_ref, x4_ref, x16_ref)
    s_len = x1_ref.shape[0]
    heads = LANES // ATTN_HEAD_DIM
    slab_w = ATTN_SLABS * LANES

    @pl.when(pl.program_id(1) == 0)
    def _():
        for di, dil in enumerate(ATTN_DILATIONS):
            _, nq, nk, _ = _attn_geometry(s_len, dil)
            rel0 = lax.broadcasted_iota(jnp.int32, (nq, nk), 1) - lax.broadcasted_iota(jnp.int32, (nq, nk), 0)
            for pos, offset in enumerate((0, -ATTN_HALF, nq - nk)):
                dist = jnp.abs(rel0 + offset).astype(F32)
                for h in range(heads):
                    pen = -(sl_ref[h:h + 1, :nk] * (float(dil) * dist))
                    bias_s[di, pos, h, :nq, :nk] = jnp.where(dist <= float(ATTN_HALF), pen, -jnp.inf)

    def slab(item, rows, i):
        di, _, _, r = item
        return x_refs[di][rows, r * slab_w + i * LANES: r * slab_w + (i + 1) * LANES]

    def rows_of(item):
        _, dil, bi, r = item
        length, nq, nk, n_blk = _attn_geometry(s_len, dil)
        q0 = bi * nq
        k0 = min(max(q0 - ATTN_HALF, 0), length - nk)
        pos = 0 if bi == 0 else (2 if bi == n_blk - 1 else 1)
        out_rows = pl.ds(r + dil * q0, nq, stride=dil) if dil > 1 else pl.ds(q0, nq)
        return pl.ds(q0, nq), pl.ds(k0, nk), out_rows, pos, nq, nk

    def scores(item):
        q_rows, k_rows, _, pos, nq, nk = rows_of(item)
        kb = slab(item, k_rows, _K)
        return [_dot_nt(slab(item, q_rows, (_Q0, _Q1)[h]), kb) + bias_s[item[0], pos, h, :nq, :nk]
                for h in range(heads)]

    def probs(s):
        m = jnp.max(s, axis=-1, keepdims=True)
        return m, jnp.exp(s - m).astype(BF16)

    def finish(item, stats):
        _, k_rows, out_rows, _, nq, _ = rows_of(item)
        head0 = lax.broadcasted_iota(jnp.int32, (nq, LANES), 1) < ATTN_HEAD_DIM
        acc = [_dot(p, slab(item, k_rows, (_V0, _V1)[h])) for h, (_, p) in enumerate(stats)]
        m_s[item[0], out_rows, :] = jnp.where(head0, stats[0][0], stats[1][0])
        num_s[item[0], out_rows, :] = jnp.where(head0, acc[0], acc[1])
        den_s[item[0], out_rows, :] = pltpu.roll(jnp.where(head0, acc[1], acc[0]), ATTN_HEAD_DIM, 1)

    items = [(di, dil, bi, r) for di, dil in enumerate(ATTN_DILATIONS)
             for bi in range(_attn_geometry(s_len, dil)[3]) for r in range(dil)]
    groups = [items[i:i + ATTN_BLOCKS_PER_BODY] for i in range(0, len(items), ATTN_BLOCKS_PER_BODY)]
    pending = None
    for group in groups + [None]:
        upcoming = [(item, scores(item)) for item in group] if group else None
        if pending:
            for item, ss in pending:
                finish(item, [probs(s) for s in ss])
        pending = upcoming

    rows_per_step = ATTN_QBLK

    def merge(ti, carry):
        rows = pl.ds(pl.multiple_of(ti * rows_per_step, rows_per_step), rows_per_step)
        ms = [m_s[di, rows, :] for di in range(len(ATTN_DILATIONS))]
        m_tot = functools.reduce(jnp.maximum, ms)
        ws = [jnp.exp(m - m_tot) for m in ms]
        num = sum(w * num_s[di, rows, :] for di, w in enumerate(ws))
        den = sum(w * den_s[di, rows, :] for di, w in enumerate(ws))
        o_ref[rows, :] = (num / den).astype(o_ref.dtype)
        return carry

    lax.fori_loop(0, s_len // rows_per_step, merge, 0)


def _attn(p_views, slopes):
    b, pairs, s, slab_w = p_views[0].shape
    heads = LANES // ATTN_HEAD_DIM
    n_dil = len(ATTN_DILATIONS)
    nk_max = ATTN_QBLK + 2 * ATTN_HALF

    def view_spec(dil):
        return pl.BlockSpec((None, None, s // dil, dil * slab_w), lambda hi, bi: (bi, hi, 0, 0))

    return pl.pallas_call(
        _attn_body,
        grid=(pairs, b),
        in_specs=[view_spec(d) for d in ATTN_DILATIONS]
                 + [pl.BlockSpec((None, 8, nk_max), lambda hi, bi: (hi, 0, 0))],
        out_specs=pl.BlockSpec((None, s, LANES), lambda hi, bi: (bi, 0, hi)),
        out_shape=jax.ShapeDtypeStruct((b, s, ATTN_WIDTH), BF16),
        scratch_shapes=[pltpu.VMEM((n_dil, 3, heads, ATTN_QBLK, nk_max), F32),
                        pltpu.VMEM((n_dil, s, LANES), F32),
                        pltpu.VMEM((n_dil, s, LANES), F32),
                        pltpu.VMEM((n_dil, s, LANES), F32)],
        compiler_params=pltpu.CompilerParams(dimension_semantics=("arbitrary", "arbitrary"),
                                             vmem_limit_bytes=VMEM_LIMIT),
        name="dilated_attn",
    )(*p_views, slopes)


def _memkv_body(m_ref, g_ref, w_ref, k_ref, v_ref):
    mn = _rms(m_ref[...], g_ref[...]).astype(BF16)
    d = k_ref.shape[1]
    k_ref[...] = _dot(mn, w_ref[:, :d]).astype(BF16)
    v_ref[...] = _dot(mn, w_ref[:, d:]).astype(BF16)


def _memkv(mem, g, w_kv):
    b, m, d = mem.shape
    blk = pl.BlockSpec((None, m, d), lambda bi: (bi, 0, 0))
    return pl.pallas_call(
        _memkv_body,
        grid=(b,),
        in_specs=[blk, _resident((1, d)), _resident(w_kv.shape)],
        out_specs=[blk, blk],
        out_shape=[jax.ShapeDtypeStruct((b, m, d), BF16)] * 2,
        compiler_params=pltpu.CompilerParams(dimension_semantics=("arbitrary",), vmem_limit_bytes=VMEM_LIMIT),
        name="mem_kv",
    )(mem, g, w_kv)


def _mix_xattn_body(h_ref, yh_ref, ya_ref, wo_h_ref, wo_a_ref, g_ref, wq_ref, k_ref, v_ref, wxo_ref, o_ref):
    h = h_ref[...] + _dot(yh_ref[...], wo_h_ref[...]) + _dot(ya_ref[...], wo_a_ref[...])
    d = h.shape[1]
    dh = d // MEM_HEADS
    hn = _rms(h, g_ref[...]).astype(BF16)
    q = (_dot(hn, wq_ref[...]) * (1.0 / math.sqrt(dh))).astype(BF16)
    outs = []
    for hd in range(MEM_HEADS):
        cols = slice(hd * dh, (hd + 1) * dh)
        s = _dot_nt(q[:, cols], k_ref[:, cols])
        p = jnp.exp(s - jnp.max(s, axis=-1, keepdims=True))
        l = jnp.sum(p, axis=-1, keepdims=True)
        outs.append((_dot(p.astype(BF16), v_ref[:, cols]) / l).astype(BF16))
    o_ref[...] = h + _dot(jnp.concatenate(outs, axis=1), wxo_ref[...])


def _mix_xattn(h, y_h, y_a, wo_h, wo_a, g, wq, k, v, wxo, *, tm=512):
    b, s, d = h.shape
    m = k.shape[1]

    def rows(w):
        return pl.BlockSpec((None, tm, w), lambda bi, i: (bi, i, 0))

    mem = pl.BlockSpec((None, m, d), lambda bi, i: (bi, 0, 0))
    return pl.pallas_call(
        _mix_xattn_body,
        grid=(b, s // tm),
        in_specs=[rows(d), rows(y_h.shape[2]), rows(y_a.shape[2]), _resident(wo_h.shape), _resident(wo_a.shape),
                  _resident((1, d)), _resident(wq.shape), mem, mem, _resident(wxo.shape)],
        out_specs=rows(d),
        out_shape=jax.ShapeDtypeStruct((b, s, d), F32),
        compiler_params=pltpu.CompilerParams(dimension_semantics=("arbitrary", "arbitrary"),
                                             vmem_limit_bytes=VMEM_LIMIT),
        name="mix_out_xattn",
    )(h, y_h, y_a, wo_h, wo_a, g, wq, k, v, wxo)


def _hgrn_consts(lb_fwd, lb_bwd, out_gain):
    rows = []
    for lb in (lb_fwd, lb_bwd):
        rows += [jnp.log(lb) * LOG2_E, jnp.log1p(-lb) * LOG2_E, 1.0 - lb]
    rows += [out_gain, jnp.ones_like(out_gain)]
    return jnp.stack(rows).astype(F32)


def _attn_slopes():
    sl = np.array([2.0 ** (-8.0 * (h + 1) / ATTN_HEADS) for h in range(ATTN_HEADS)], np.float32)
    per_pair = LANES // ATTN_HEAD_DIM
    out = np.zeros((ATTN_HEADS // per_pair, 8, ATTN_QBLK + 2 * ATTN_HALF), np.float32)
    for h in range(ATTN_HEADS):
        out[h // per_pair, h % per_pair, :] = sl[h]
    return jnp.asarray(out)


def kernel(x, mem, ln_ffn1, ffn1_w_gate, ffn1_w_up, ffn1_w_down, ln_mix, w_in, hgrn_lb_logits, hgrn_out_norm,
           w_out, ln_xq, ln_mem, w_xq, w_xkv, w_xo, ln_ffn2, ffn2_w_gate, ffn2_w_up, ffn2_w_down, ln_final):
    b, s, d = x.shape
    depth = ln_ffn1.shape[0]
    assert s % (ATTN_DILATIONS[-1] * 2 * ATTN_HALF) == 0 and s % ATTN_QBLK == 0 and s % HGRN_CHUNK == 0

    lb_all = jnp.cumsum(jax.nn.softmax(hgrn_lb_logits.astype(F32), axis=0), axis=0)
    lb_all = lb_all - lb_all[0:1]
    cum, masks = _hgrn_tables(HGRN_CHUNK)
    slopes = _attn_slopes()
    bf = lambda w: w.astype(BF16)
    vec = lambda g: g.astype(F32).reshape(1, -1)

    h = x.reshape(b * s, d)
    for l in range(depth):
        h = _ffn(h, vec(ln_ffn1[l]), bf(ffn1_w_gate[l]), bf(ffn1_w_up[l]), bf(ffn1_w_down[l]))
        p_h, *p_views = _proj(h.reshape(b, s, d), vec(ln_mix[l]), bf(w_in[l]))
        y_h = _hgrn(p_h, _hgrn_consts(lb_all[l, 0], lb_all[l, 1], hgrn_out_norm[l]), cum, masks)
        y_a = _attn(p_views, slopes)
        k_mem, v_mem = _memkv(mem, vec(ln_mem[l]), bf(w_xkv[l]))
        wo = bf(w_out[l])
        h = _mix_xattn(h.reshape(b, s, d), y_h, y_a, wo[:HGRN_WIDTH], wo[HGRN_WIDTH:], vec(ln_xq[l]),
                       bf(w_xq[l]), k_mem, v_mem, bf(w_xo[l])).reshape(b * s, d)
        last = l == depth - 1
        h = _ffn(h, vec(ln_ffn2[l]), bf(ffn2_w_gate[l]), bf(ffn2_w_up[l]), bf(ffn2_w_down[l]),
                 vec(ln_final) if last else None)
    return h.reshape(b, s, d)
```

```python
import functools
import math

import numpy as np
import jax
import jax.numpy as jnp
from jax import lax
from jax.experimental import pallas as pl
from jax.experimental.pallas import tpu as pltpu

F32 = jnp.float32
BF16 = jnp.bfloat16

EPS = 1e-6
FFN_RES = 0.5
LOG2_E = math.log2(math.e)

HGRN_HEAD_DIM = 128
HGRN_HEADS = 4
HGRN_WIDTH = HGRN_HEAD_DIM * HGRN_HEADS
HGRN_PARTS = 5
HGRN_CHUNK = 64
HGRN_UNROLL = 4
ATTN_HEAD_DIM = 64
ATTN_HEADS = 8
ATTN_WIDTH = ATTN_HEAD_DIM * ATTN_HEADS
ATTN_HALF = 64
ATTN_DILATIONS = (1, 4, 16)
ATTN_QBLK = 128
ATTN_BLOCKS_PER_BODY = 2
MEM_HEADS = 4

LANES = 128
MXU_DEPTH = 256
VMEM_LIMIT = 56 * 1024 * 1024

TOKEN_TILE = 512
DENSE_TOKEN_TILE = 1024
FFN_HIDDEN_CHUNK = 6 * MXU_DEPTH


def _rms(x, g):
    return x * lax.rsqrt(jnp.mean(x * x, axis=-1, keepdims=True) + EPS) * g


def _dot(a, b):
    return jnp.dot(a, b, preferred_element_type=F32)


def _dot_nt(a, b):
    return lax.dot_general(a, b, (((1,), (1,)), ((), ())), preferred_element_type=F32)


def _dot_tn(a, b):
    return lax.dot_general(a, b, (((0,), (0,)), ((), ())), preferred_element_type=F32)


def _resident(shape):
    return pl.BlockSpec(shape, lambda *_: (0,) * len(shape), pipeline_mode=pl.Buffered(1))


def _ffn_body(x_ref, g_ref, wg_ref, wu_ref, wd_ref, *rest, final_norm):
    o_ref = rest[-1]
    x = x_ref[...]
    xn = _rms(x, g_ref[...]).astype(BF16)
    ffn = None
    f = wg_ref.shape[1]
    for lo in range(0, f, FFN_HIDDEN_CHUNK):
        cols = slice(lo, min(lo + FFN_HIDDEN_CHUNK, f))
        gate = _dot(xn, wg_ref[:, cols])
        up = _dot(xn, wu_ref[:, cols])
        act = (gate * jax.nn.sigmoid(gate) * up).astype(BF16)
        part = _dot(act, wd_ref[cols, :])
        ffn = part if ffn is None else ffn + part
    y = x + FFN_RES * ffn
    if final_norm:
        y = _rms(y, rest[0][...])
    o_ref[...] = y


def _ffn(h, g, wg, wu, wd, g_final=None):
    n, d = h.shape
    tm = DENSE_TOKEN_TILE
    f = wg.shape[1]
    final_norm = g_final is not None
    row = pl.BlockSpec((tm, d), lambda i: (i, 0))
    in_specs = [row, _resident((1, d)), _resident((d, f)), _resident((d, f)), _resident((f, d))]
    args = [h, g, wg, wu, wd]
    if final_norm:
        in_specs.append(_resident((1, d)))
        args.append(g_final)
    return pl.pallas_call(
        functools.partial(_ffn_body, final_norm=final_norm),
        grid=(n // tm,),
        in_specs=in_specs,
        out_specs=row,
        out_shape=jax.ShapeDtypeStruct((n, d), F32),
        compiler_params=pltpu.CompilerParams(dimension_semantics=("arbitrary",), vmem_limit_bytes=VMEM_LIMIT),
        name="ffn",
    )(*args)


ATTN_SLABS = 5
_Q0, _Q1, _K, _V0, _V1 = range(ATTN_SLABS)


def _proj_body(x_ref, g_ref, w_ref, oh_ref, *rest):
    oa_refs, stage = rest[:-1], rest[-1]
    xn = _rms(x_ref[...], g_ref[...]).astype(BF16)
    n_h = oh_ref.shape[1]
    pa = _dot(xn, w_ref[:, n_h:])
    oh_ref[...] = _dot(xn, w_ref[:, :n_h])
    tm = pa.shape[0]
    slab_w = ATTN_SLABS * LANES
    lane = lax.broadcasted_iota(jnp.int32, (tm, LANES), 1)
    head0 = lane < ATTN_HEAD_DIM
    for hp in range(ATTN_WIDTH // LANES):
        q, k, v = (pa[:, p * ATTN_WIDTH + hp * LANES: p * ATTN_WIDTH + (hp + 1) * LANES] for p in range(3))
        q = q * (1.0 / math.sqrt(ATTN_HEAD_DIM))
        slabs = {
            _Q0: jnp.where(head0, q, 0.0),
            _Q1: jnp.where(head0, 0.0, q),
            _K: k,
            _V0: jnp.where(head0, v, 1.0),
            _V1: jnp.where(head0, 1.0, v),
        }
        for i, val in slabs.items():
            stage[0, hp * ATTN_SLABS + i] = val
            oa_refs[0][hp, :, i * LANES:(i + 1) * LANES] = val.astype(BF16)
    for lvl in range(1, len(ATTN_DILATIONS)):
        prev, dil = ATTN_DILATIONS[lvl - 1], ATTN_DILATIONS[lvl]
        step = dil // prev
        n_prev, n_rows = tm // prev, tm // dil
        for si in range(stage.shape[1]):
            hp, i = divmod(si, ATTN_SLABS)
            for r in range(dil):
                hi, lo = divmod(r, prev)
                x = stage[lvl - 1, si, pl.ds(lo * n_prev + hi, n_rows, stride=step), :]
                oa_refs[lvl][hp, :, r * slab_w + i * LANES: r * slab_w + (i + 1) * LANES] = x.astype(BF16)
                if lvl + 1 < len(ATTN_DILATIONS):
                    stage[lvl, si, r * n_rows:(r + 1) * n_rows, :] = x


def _proj(h, g, w_in):
    b, s, d = h.shape
    tm = TOKEN_TILE
    n_h = HGRN_PARTS * HGRN_WIDTH
    pairs = ATTN_WIDTH // LANES
    slab_w = ATTN_SLABS * LANES
    return pl.pallas_call(
        _proj_body,
        grid=(b, s // tm),
        in_specs=[pl.BlockSpec((None, tm, d), lambda bi, i: (bi, i, 0)), _resident((1, d)), _resident(w_in.shape)],
        out_specs=[pl.BlockSpec((None, tm, n_h), lambda bi, i: (bi, i, 0))]
                  + [pl.BlockSpec((None, pairs, tm // dil, dil * slab_w), lambda bi, i: (bi, 0, i, 0))
                     for dil in ATTN_DILATIONS],
        out_shape=[jax.ShapeDtypeStruct((b, s, n_h), F32)]
                  + [jax.ShapeDtypeStruct((b, pairs, s // dil, dil * slab_w), BF16) for dil in ATTN_DILATIONS],
        scratch_shapes=[pltpu.VMEM((len(ATTN_DILATIONS) - 1, pairs * ATTN_SLABS, tm, LANES), F32)],
        compiler_params=pltpu.CompilerParams(dimension_semantics=("arbitrary", "arbitrary"),
                                             vmem_limit_bytes=VMEM_LIMIT),
        name="mix_proj",
    )(h, g, w_in)


_HGRN_LEVELS = tuple(2 ** e for e in range(int(math.log2(HGRN_CHUNK))))


def _hgrn_tables(c):
    t = np.arange(c)[:, None]
    s = np.arange(c)[None, :]
    cum = np.stack([(s <= t), (s >= t)]).astype(np.float32)
    cum = jnp.asarray(cum, BF16)
    masks = np.zeros((2, len(_HGRN_LEVELS) + 1, c, c), np.float32)
    masks[:, 0] = np.eye(c)
    for li, m in enumerate(_HGRN_LEVELS):
        same = (t // (2 * m)) == (s // (2 * m))
        t_hi, s_hi = (t % (2 * m)) >= m, (s % (2 * m)) >= m
        masks[0, li + 1] = same & t_hi & ~s_hi
        masks[1, li + 1] = same & ~t_hi & s_hi
    return jnp.asarray(cum), jnp.asarray(masks, BF16)


def _neg_abs(x):
    bits = lax.bitcast_convert_type(x, jnp.uint32) | jnp.uint32(0x80000000)
    return lax.bitcast_convert_type(bits, F32)


def _level_exponent(x, lf, row, m, fwd):
    c, w = x.shape
    if m >= 4:
        r = m - 1 if fwd else m
        xb = x.reshape(c // (2 * m), 2 * m, w)
        return _neg_abs(x - jnp.broadcast_to(xb[:, r:r + 1, :], xb.shape).reshape(c, w))
    j = row % (2 * m)
    after, before = pltpu.roll(lf, c - 1, 0), pltpu.roll(lf, 1, 0)
    if m == 1:
        return jnp.where(j == (1 if fwd else 0), lf, 0.0)
    assert m == 2
    if fwd:
        return jnp.where(j == 0, after, jnp.where(j == 1, 0.0, jnp.where(j == 2, lf, lf + before)))
    return jnp.where(j == 0, lf + after, jnp.where(j == 1, lf, jnp.where(j == 2, 0.0, before)))


def _hgrn_gates(z, log2_lb, log2_1m_lb, one_m_lb, one):
    z2 = z * LOG2_E
    log2_sig = jnp.minimum(z2, 0.0) - jnp.log2(one + jnp.exp2(_neg_abs(z2)))
    b = log2_1m_lb + log2_sig
    log2_f = jnp.maximum(log2_lb, b) + jnp.log2(one + jnp.exp2(_neg_abs(log2_lb - b)))
    return log2_f, one_m_lb * jnp.exp2(log2_sig - z2)


def _cumsum_rows(cum_bf, x):
    w = x.shape[1]
    hi = x.astype(BF16)
    rest = x - hi.astype(F32)
    mid = rest.astype(BF16)
    lo = (rest - mid.astype(F32)).astype(BF16)
    y = _dot(cum_bf, jnp.concatenate([hi, mid, lo], axis=1))
    return y[:, :w] + y[:, w:2 * w] + y[:, 2 * w:]


def _hgrn_group(items, states, cum_ref, mask_ref):
    c = HGRN_CHUNK
    xs = [_cumsum_rows(cum_ref[d], log2_f) for _, _, _, log2_f, d in items]
    row = lax.broadcasted_iota(jnp.int32, xs[0].shape, 0)
    q_bf = [q.astype(BF16) for q, _, _, _, _ in items]
    k_bf = [k.astype(BF16) for _, k, _, _, _ in items]
    v_bf = [v.astype(BF16) for _, _, v, _, _ in items]
    masks = [mask_ref[d, 0] for d in (0, 1)]
    scores = [_dot_nt(q, k).astype(BF16) * masks[d] for q, k, (_, _, _, _, d) in zip(q_bf, k_bf, items)]
    for li, m in enumerate(_HGRN_LEVELS):
        masks = [mask_ref[d, li + 1] for d in (0, 1)]
        for j, (_, _, _, log2_f, d) in enumerate(items):
            e = jnp.exp2(_level_exponent(xs[j], log2_f, row, m, d == 0)).astype(BF16)
            scores[j] = scores[j] + _dot_nt(q_bf[j] * e, k_bf[j] * e).astype(BF16) * masks[d]
    outs = [_dot(sc, v) for sc, v in zip(scores, v_bf)]
    edges = [x[c - 1:c, :] if d == 0 else x[0:1, :] for x, (_, _, _, _, d) in zip(xs, items)]
    handed = [_dot_tn(v, (k * jnp.exp2(edge - x)).astype(BF16))
              for v, edge, x, (_, k, _, _, _) in zip(v_bf, edges, xs, items)]
    q_in = [(q * jnp.exp2(x)).astype(BF16) for x, (q, _, _, _, _) in zip(xs, items)]
    states = list(states)
    for j, (_, _, _, _, d) in enumerate(items):
        outs[j] = outs[j] + _dot_nt(q_in[j], states[d].astype(BF16))
        states[d] = states[d] * jnp.exp2(edges[j]) + handed[j]
    return outs, states


def _hgrn_body(q_ref, i_ref, zf_ref, zb_ref, g_ref, c_ref, cum_ref, mask_ref, o_ref, acc_ref):
    s, dk = q_ref.shape
    c = HGRN_CHUNK
    n = s // c
    consts = c_ref[...]
    acc_ref[...] = jnp.zeros_like(acc_ref)
    z_refs = (zf_ref, zb_ref)

    def step(gi, states):
        rows, items = [], []
        for u in range(HGRN_UNROLL):
            for d in (0, 1):
                ci = gi * HGRN_UNROLL + u
                r = pl.ds(pl.multiple_of((ci if d == 0 else n - 1 - ci) * c, c), c)
                log2_f, k = _hgrn_gates(z_refs[d][r, :], *(consts[3 * d + i:3 * d + i + 1] for i in range(3)),
                                        consts[7:8])
                rows.append(r)
                items.append((q_ref[r, :], k, i_ref[r, :], log2_f, d))
        outs, states = _hgrn_group(items, states, cum_ref, mask_ref)
        for r, o in zip(rows, outs):
            acc_ref[r, :] += o
        return tuple(states)

    zero = jnp.zeros((dk, dk), F32)
    lax.fori_loop(0, n // HGRN_UNROLL, step, (zero, zero), unroll=True)

    o = acc_ref[...]
    o = o * lax.rsqrt(jnp.mean(o * o, axis=-1, keepdims=True) + EPS) * consts[6:7]
    g = g_ref[...]
    o_ref[...] = (o * (g * jax.nn.sigmoid(g))).astype(o_ref.dtype)


def _hgrn(p_h, consts, cum, masks):
    b, s, _ = p_h.shape
    dk = HGRN_HEAD_DIM

    def part(p):
        return pl.BlockSpec((None, s, dk), lambda bi, hi, p=p: (bi, 0, p * HGRN_HEADS + hi))

    return pl.pallas_call(
        _hgrn_body,
        grid=(b, HGRN_HEADS),
        in_specs=[part(0), part(1), part(2), part(3), part(4),
                  pl.BlockSpec((8, dk), lambda bi, hi: (0, hi)),
                  _resident(cum.shape), _resident(masks.shape)],
        out_specs=pl.BlockSpec((None, s, dk), lambda bi, hi: (bi, 0, hi)),
        out_shape=jax.ShapeDtypeStruct((b, s, HGRN_WIDTH), BF16),
        scratch_shapes=[pltpu.VMEM((s, dk), F32)],
        compiler_params=pltpu.CompilerParams(dimension_semantics=("arbitrary", "arbitrary"),
                                             vmem_limit_bytes=VMEM_LIMIT),
        name="hgrn2",
    )(p_h, p_h, p_h, p_h, p_h, consts, cum, masks)


def _attn_geometry(s_len, dil):
    length = s_len // dil
    nq = min(ATTN_QBLK, length)
    nk = min(nq + 2 * ATTN_HALF, length)
    return length, nq, nk, length // nq


def _attn_body(x1_ref, x4_ref, x16_ref, sl_ref, o_ref, bias_s, m_s, num_s, den_s):
    x_refs = (x1_ref, x4_ref, x16_ref)
    s_len = x1_ref.shape[0]
    heads = LANES // ATTN_HEAD_DIM
    slab_w = ATTN_SLABS * LANES

    @pl.when(pl.program_id(1) == 0)
    def _():
        for di, dil in enumerate(ATTN_DILATIONS):
            _, nq, nk, _ = _attn_geometry(s_len, dil)
            rel0 = lax.broadcasted_iota(jnp.int32, (nq, nk), 1) - lax.broadcasted_iota(jnp.int32, (nq, nk), 0)
            for pos, offset in enumerate((0, -ATTN_HALF, nq - nk)):
                dist = jnp.abs(rel0 + offset).astype(F32)
                for h in range(heads):
                    pen = -(sl_ref[h:h + 1, :nk] * (float(dil) * dist))
                    bias_s[di, pos, h, :nq, :nk] = jnp.where(dist <= float(ATTN_HALF), pen, -jnp.inf)

    def slab(item, rows, i):
        di, _, _, r = item
        return x_refs[di][rows, r * slab_w + i * LANES: r * slab_w + (i + 1) * LANES]

    def rows_of(item):
        _, dil, bi, r = item
        length, nq, nk, n_blk = _attn_geometry(s_len, dil)
        q0 = bi * nq
        k0 = min(max(q0 - ATTN_HALF, 0), length - nk)
        pos = 0 if bi == 0 else (2 if bi == n_blk - 1 else 1)
        out_rows = pl.ds(r + dil * q0, nq, stride=dil) if dil > 1 else pl.ds(q0, nq)
        return pl.ds(q0, nq), pl.ds(k0, nk), out_rows, pos, nq, nk

    def scores(item):
        q_rows, k_rows, _, pos, nq, nk = rows_of(item)
        kb = slab(item, k_rows, _K)
        return [_dot_nt(slab(item, q_rows, (_Q0, _Q1)[h]), kb) + bias_s[item[0], pos, h, :nq, :nk]
                for h in range(heads)]

    def probs(s):
        m = jnp.max(s, axis=-1, keepdims=True)
        return m, jnp.exp(s - m).astype(BF16)

    def finish(item, stats):
        _, k_rows, out_rows, _, nq, _ = rows_of(item)
        head0 = lax.broadcasted_iota(jnp.int32, (nq, LANES), 1) < ATTN_HEAD_DIM
        acc = [_dot(p, slab(item, k_rows, (_V0, _V1)[h])) for h, (_, p) in enumerate(stats)]
        m_s[item[0], out_rows, :] = jnp.where(head0, stats[0][0], stats[1][0])
        num_s[item[0], out_rows, :] = jnp.where(head0, acc[0], acc[1])
        den_s[item[0], out_rows, :] = pltpu.roll(jnp.where(head0, acc[1], acc[0]), ATTN_HEAD_DIM, 1)

    items = [(di, dil, bi, r) for di, dil in enumerate(ATTN_DILATIONS)
             for bi in range(_attn_geometry(s_len, dil)[3]) for r in range(dil)]
    groups = [items[i:i + ATTN_BLOCKS_PER_BODY] for i in range(0, len(items), ATTN_BLOCKS_PER_BODY)]
    pending = None
    for group in groups + [None]:
        upcoming = [(item, scores(item)) for item in group] if group else None
        if pending:
            for item, ss in pending:
                finish(item, [probs(s) for s in ss])
        pending = upcoming

    rows_per_step = ATTN_QBLK

    def merge(ti, carry):
        rows = pl.ds(pl.multiple_of(ti * rows_per_step, rows_per_step), rows_per_step)
        ms = [m_s[di, rows, :] for di in range(len(ATTN_DILATIONS))]
        m_tot = functools.reduce(jnp.maximum, ms)
        ws = [jnp.exp(m - m_tot) for m in ms]
        num = sum(w * num_s[di, rows, :] for di, w in enumerate(ws))
        den = sum(w * den_s[di, rows, :] for di, w in enumerate(ws))
        o_ref[rows, :] = (num / den).astype(o_ref.dtype)
        return carry

    lax.fori_loop(0, s_len // rows_per_step, merge, 0)


def _attn(p_views, slopes):
    b, pairs, s, slab_w = p_views[0].shape
    heads = LANES // ATTN_HEAD_DIM
    n_dil = len(ATTN_DILATIONS)
    nk_max = ATTN_QBLK + 2 * ATTN_HALF

    def view_spec(dil):
        return pl.BlockSpec((None, None, s // dil, dil * slab_w), lambda hi, bi: (bi, hi, 0, 0))

    return pl.pallas_call(
        _attn_body,
        grid=(pairs, b),
        in_specs=[view_spec(d) for d in ATTN_DILATIONS]
                 + [pl.BlockSpec((None, 8, nk_max), lambda hi, bi: (hi, 0, 0))],
        out_specs=pl.BlockSpec((None, s, LANES), lambda hi, bi: (bi, 0, hi)),
        out_shape=jax.ShapeDtypeStruct((b, s, ATTN_WIDTH), BF16),
        scratch_shapes=[pltpu.VMEM((n_dil, 3, heads, ATTN_QBLK, nk_max), F32),
                        pltpu.VMEM((n_dil, s, LANES), F32),
                        pltpu.VMEM((n_dil, s, LANES), F32),
                        pltpu.VMEM((n_dil, s, LANES), F32)],
        compiler_params=pltpu.CompilerParams(dimension_semantics=("arbitrary", "arbitrary"),
                                             vmem_limit_bytes=VMEM_LIMIT),
        name="dilated_attn",
    )(*p_views, slopes)


def _memkv_body(m_ref, g_ref, w_ref, k_ref, v_ref):
    mn = _rms(m_ref[...], g_ref[...]).astype(BF16)
    d = k_ref.shape[1]
    k_ref[...] = _dot(mn, w_ref[:, :d]).astype(BF16)
    v_ref[...] = _dot(mn, w_ref[:, d:]).astype(BF16)


def _memkv(mem, g, w_kv):
    b, m, d = mem.shape
    blk = pl.BlockSpec((None, m, d), lambda bi: (bi, 0, 0))
    return pl.pallas_call(
        _memkv_body,
        grid=(b,),
        in_specs=[blk, _resident((1, d)), _resident(w_kv.shape)],
        out_specs=[blk, blk],
        out_shape=[jax.ShapeDtypeStruct((b, m, d), BF16)] * 2,
        compiler_params=pltpu.CompilerParams(dimension_semantics=("arbitrary",), vmem_limit_bytes=VMEM_LIMIT),
        name="mem_kv",
    )(mem, g, w_kv)


def _mix_xattn_body(h_ref, yh_ref, ya_ref, wo_h_ref, wo_a_ref, g_ref, wq_ref, k_ref, v_ref, wxo_ref, o_ref):
    h = h_ref[...] + _dot(yh_ref[...], wo_h_ref[...]) + _dot(ya_ref[...], wo_a_ref[...])
    d = h.shape[1]
    dh = d // MEM_HEADS
    hn = _rms(h, g_ref[...]).astype(BF16)
    q = (_dot(hn, wq_ref[...]) * (1.0 / math.sqrt(dh))).astype(BF16)
    outs = []
    for hd in range(MEM_HEADS):
        cols = slice(hd * dh, (hd + 1) * dh)
        s = _dot_nt(q[:, cols], k_ref[:, cols])
        p = jnp.exp(s - jnp.max(s, axis=-1, keepdims=True))
        l = jnp.sum(p, axis=-1, keepdims=True)
        outs.append((_dot(p.astype(BF16), v_ref[:, cols]) / l).astype(BF16))
    o_ref[...] = h + _dot(jnp.concatenate(outs, axis=1), wxo_ref[...])


def _mix_xattn(h, y_h, y_a, wo_h, wo_a, g, wq, k, v, wxo):
    b, s, d = h.shape
    tm = DENSE_TOKEN_TILE
    m = k.shape[1]

    def rows(w):
        return pl.BlockSpec((None, tm, w), lambda bi, i: (bi, i, 0))

    mem = pl.BlockSpec((None, m, d), lambda bi, i: (bi, 0, 0))
    return pl.pallas_call(
        _mix_xattn_body,
        grid=(b, s // tm),
        in_specs=[rows(d), rows(y_h.shape[2]), rows(y_a.shape[2]), _resident(wo_h.shape), _resident(wo_a.shape),
                  _resident((1, d)), _resident(wq.shape), mem, mem, _resident(wxo.shape)],
        out_specs=rows(d),
        out_shape=jax.ShapeDtypeStruct((b, s, d), F32),
        compiler_params=pltpu.CompilerParams(dimension_semantics=("arbitrary", "arbitrary"),
                                             vmem_limit_bytes=VMEM_LIMIT),
        name="mix_out_xattn",
    )(h, y_h, y_a, wo_h, wo_a, g, wq, k, v, wxo)


def _hgrn_consts(lb_fwd, lb_bwd, out_gain):
    rows = []
    for lb in (lb_fwd, lb_bwd):
        rows += [jnp.log(lb) * LOG2_E, jnp.log1p(-lb) * LOG2_E, 1.0 - lb]
    rows += [out_gain, jnp.ones_like(out_gain)]
    return jnp.stack(rows).astype(F32)


def _attn_slopes():
    sl = np.array([2.0 ** (-8.0 * (h + 1) / ATTN_HEADS) for h in range(ATTN_HEADS)], np.float32)
    per_pair = LANES // ATTN_HEAD_DIM
    out = np.zeros((ATTN_HEADS // per_pair, 8, ATTN_QBLK + 2 * ATTN_HALF), np.float32)
    for h in range(ATTN_HEADS):
        out[h // per_pair, h % per_pair, :] = sl[h]
    return jnp.asarray(out)


def kernel(x, mem, ln_ffn1, ffn1_w_gate, ffn1_w_up, ffn1_w_down, ln_mix, w_in, hgrn_lb_logits, hgrn_out_norm,
           w_out, ln_xq, ln_mem, w_xq, w_xkv, w_xo, ln_ffn2, ffn2_w_gate, ffn2_w_up, ffn2_w_down, ln_final):
    b, s, d = x.shape
    depth = ln_ffn1.shape[0]
    assert s % (ATTN_DILATIONS[-1] * 2 * ATTN_HALF) == 0 and s % ATTN_QBLK == 0
    assert s % (HGRN_CHUNK * HGRN_UNROLL) == 0 and s % TOKEN_TILE == 0 and s % DENSE_TOKEN_TILE == 0

    lb_all = jnp.cumsum(jax.nn.softmax(hgrn_lb_logits.astype(F32), axis=0), axis=0)
    lb_all = lb_all - lb_all[0:1]
    cum, masks = _hgrn_tables(HGRN_CHUNK)
    slopes = _attn_slopes()
    bf = lambda w: w.astype(BF16)
    vec = lambda g: g.astype(F32).reshape(1, -1)

    h = x.reshape(b * s, d)
    for l in range(depth):
        h = _ffn(h, vec(ln_ffn1[l]), bf(ffn1_w_gate[l]), bf(ffn1_w_up[l]), bf(ffn1_w_down[l]))
        p_h, *p_views = _proj(h.reshape(b, s, d), vec(ln_mix[l]), bf(w_in[l]))
        y_h = _hgrn(p_h, _hgrn_consts(lb_all[l, 0], lb_all[l, 1], hgrn_out_norm[l]), cum, masks)
        y_a = _attn(p_views, slopes)
        k_mem, v_mem = _memkv(mem, vec(ln_mem[l]), bf(w_xkv[l]))
        wo = bf(w_out[l])
        h = _mix_xattn(h.reshape(b, s, d), y_h, y_a, wo[:HGRN_WIDTH], wo[HGRN_WIDTH:], vec(ln_xq[l]),
                       bf(w_xq[l]), k_mem, v_mem, bf(w_xo[l])).reshape(b * s, d)
        last = l == depth - 1
        h = _ffn(h, vec(ln_ffn2[l]), bf(ffn2_w_gate[l]), bf(ffn2_w_up[l]), bf(ffn2_w_down[l]),
                 vec(ln_final) if last else None)
    return h.reshape(b, s, d)
```

```python
import functools
import math

import numpy as np
import jax
import jax.numpy as jnp
from jax import lax
from jax.experimental import pallas as pl
from jax.experimental.pallas import tpu as pltpu

F32 = jnp.float32
BF16 = jnp.bfloat16

EPS = 1e-6
FFN_RES = 0.5
LOG2_E = math.log2(math.e)

HGRN_HEAD_DIM = 128
HGRN_HEADS = 4
HGRN_WIDTH = HGRN_HEAD_DIM * HGRN_HEADS
HGRN_PARTS = 5
HGRN_CHUNK = 64
HGRN_UNROLL = 4
ATTN_HEAD_DIM = 64
ATTN_HEADS = 8
ATTN_WIDTH = ATTN_HEAD_DIM * ATTN_HEADS
ATTN_HALF = 64
ATTN_DILATIONS = (1, 4, 16)
ATTN_QBLK = 128
ATTN_BLOCKS_PER_BODY = 2
MEM_HEADS = 4

LANES = 128
MXU_DEPTH = 256
VMEM_LIMIT = 56 * 1024 * 1024

TOKEN_TILE = 512
DENSE_TOKEN_TILE = 1024
FFN_HIDDEN_CHUNK = 6 * MXU_DEPTH


def _rms(x, g):
    return x * lax.rsqrt(jnp.mean(x * x, axis=-1, keepdims=True) + EPS) * g


def _dot(a, b):
    return jnp.dot(a, b, preferred_element_type=F32)


def _dot_nt(a, b):
    return lax.dot_general(a, b, (((1,), (1,)), ((), ())), preferred_element_type=F32)


def _dot_tn(a, b):
    return lax.dot_general(a, b, (((0,), (0,)), ((), ())), preferred_element_type=F32)


def _resident(shape):
    return pl.BlockSpec(shape, lambda *_: (0,) * len(shape), pipeline_mode=pl.Buffered(1))


def _ffn_body(x_ref, g_ref, wg_ref, wu_ref, wd_ref, *rest, final_norm):
    o_ref = rest[-1]
    x = x_ref[...]
    xn = _rms(x, g_ref[...]).astype(BF16)
    ffn = None
    f = wg_ref.shape[1]
    for lo in range(0, f, FFN_HIDDEN_CHUNK):
        cols = slice(lo, min(lo + FFN_HIDDEN_CHUNK, f))
        gate = _dot(xn, wg_ref[:, cols])
        up = _dot(xn, wu_ref[:, cols])
        act = (gate * jax.nn.sigmoid(gate) * up).astype(BF16)
        part = _dot(act, wd_ref[cols, :])
        ffn = part if ffn is None else ffn + part
    y = x + FFN_RES * ffn
    if final_norm:
        y = _rms(y, rest[0][...])
    o_ref[...] = y


def _ffn(h, g, wg, wu, wd, g_final=None):
    n, d = h.shape
    tm = DENSE_TOKEN_TILE
    f = wg.shape[1]
    final_norm = g_final is not None
    row = pl.BlockSpec((tm, d), lambda i: (i, 0))
    in_specs = [row, _resident((1, d)), _resident((d, f)), _resident((d, f)), _resident((f, d))]
    args = [h, g, wg, wu, wd]
    if final_norm:
        in_specs.append(_resident((1, d)))
        args.append(g_final)
    return pl.pallas_call(
        functools.partial(_ffn_body, final_norm=final_norm),
        grid=(n // tm,),
        in_specs=in_specs,
        out_specs=row,
        out_shape=jax.ShapeDtypeStruct((n, d), F32),
        compiler_params=pltpu.CompilerParams(dimension_semantics=("arbitrary",), vmem_limit_bytes=VMEM_LIMIT),
        name="ffn",
    )(*args)


ATTN_SLABS = 5
_Q0, _Q1, _K, _V0, _V1 = range(ATTN_SLABS)


def _proj_body(x_ref, g_ref, w_ref, oh_ref, *rest):
    oa_refs, stage = rest[:-1], rest[-1]
    xn = _rms(x_ref[...], g_ref[...]).astype(BF16)
    n_h = oh_ref.shape[1]
    pa = _dot(xn, w_ref[:, n_h:])
    oh_ref[...] = _dot(xn, w_ref[:, :n_h])
    tm = pa.shape[0]
    slab_w = ATTN_SLABS * LANES
    lane = lax.broadcasted_iota(jnp.int32, (tm, LANES), 1)
    head0 = lane < ATTN_HEAD_DIM
    for hp in range(ATTN_WIDTH // LANES):
        q, k, v = (pa[:, p * ATTN_WIDTH + hp * LANES: p * ATTN_WIDTH + (hp + 1) * LANES] for p in range(3))
        q = q * (1.0 / math.sqrt(ATTN_HEAD_DIM))
        slabs = {
            _Q0: jnp.where(head0, q, 0.0),
            _Q1: jnp.where(head0, 0.0, q),
            _K: k,
            _V0: jnp.where(head0, v, 1.0),
            _V1: jnp.where(head0, 1.0, v),
        }
        for i, val in slabs.items():
            stage[0, hp * ATTN_SLABS + i] = val
            oa_refs[0][hp, :, i * LANES:(i + 1) * LANES] = val.astype(BF16)
    for lvl in range(1, len(ATTN_DILATIONS)):
        prev, dil = ATTN_DILATIONS[lvl - 1], ATTN_DILATIONS[lvl]
        step = dil // prev
        n_prev, n_rows = tm // prev, tm // dil
        for si in range(stage.shape[1]):
            hp, i = divmod(si, ATTN_SLABS)
            for r in range(dil):
                hi, lo = divmod(r, prev)
                x = stage[lvl - 1, si, pl.ds(lo * n_prev + hi, n_rows, stride=step), :]
                oa_refs[lvl][hp, :, r * slab_w + i * LANES: r * slab_w + (i + 1) * LANES] = x.astype(BF16)
                if lvl + 1 < len(ATTN_DILATIONS):
                    stage[lvl, si, r * n_rows:(r + 1) * n_rows, :] = x


def _proj(h, g, w_in):
    b, s, d = h.shape
    tm = TOKEN_TILE
    n_h = HGRN_PARTS * HGRN_WIDTH
    pairs = ATTN_WIDTH // LANES
    slab_w = ATTN_SLABS * LANES
    return pl.pallas_call(
        _proj_body,
        grid=(b, s // tm),
        in_specs=[pl.BlockSpec((None, tm, d), lambda bi, i: (bi, i, 0)), _resident((1, d)), _resident(w_in.shape)],
        out_specs=[pl.BlockSpec((None, tm, n_h), lambda bi, i: (bi, i, 0))]
                  + [pl.BlockSpec((None, pairs, tm // dil, dil * slab_w), lambda bi, i: (bi, 0, i, 0))
                     for dil in ATTN_DILATIONS],
        out_shape=[jax.ShapeDtypeStruct((b, s, n_h), F32)]
                  + [jax.ShapeDtypeStruct((b, pairs, s // dil, dil * slab_w), BF16) for dil in ATTN_DILATIONS],
        scratch_shapes=[pltpu.VMEM((len(ATTN_DILATIONS) - 1, pairs * ATTN_SLABS, tm, LANES), F32)],
        compiler_params=pltpu.CompilerParams(dimension_semantics=("arbitrary", "arbitrary"),
                                             vmem_limit_bytes=VMEM_LIMIT),
        name="mix_proj",
    )(h, g, w_in)


_HGRN_LEVELS = tuple(2 ** e for e in range(int(math.log2(HGRN_CHUNK))))


def _hgrn_tables(c):
    t = np.arange(c)[:, None]
    s = np.arange(c)[None, :]
    cum = np.stack([(s <= t), (s >= t)]).astype(np.float32)
    cum = jnp.asarray(cum, BF16)
    masks = np.zeros((2, len(_HGRN_LEVELS) + 1, c, c), np.float32)
    masks[:, 0] = np.eye(c)
    for li, m in enumerate(_HGRN_LEVELS):
        same = (t // (2 * m)) == (s // (2 * m))
        t_hi, s_hi = (t % (2 * m)) >= m, (s % (2 * m)) >= m
        masks[0, li + 1] = same & t_hi & ~s_hi
        masks[1, li + 1] = same & ~t_hi & s_hi
    return jnp.asarray(cum), jnp.asarray(masks, BF16)


def _neg_abs(x):
    return -jnp.abs(x)


def _level_exponent(x, lf, row, m, fwd):
    c, w = x.shape
    if m >= 4:
        r = m - 1 if fwd else m
        xb = x.reshape(c // (2 * m), 2 * m, w)
        return _neg_abs(x - jnp.broadcast_to(xb[:, r:r + 1, :], xb.shape).reshape(c, w))
    j = row % (2 * m)
    after, before = pltpu.roll(lf, c - 1, 0), pltpu.roll(lf, 1, 0)
    if m == 1:
        return jnp.where(j == (1 if fwd else 0), lf, 0.0)
    assert m == 2
    if fwd:
        return jnp.where(j == 0, after, jnp.where(j == 1, 0.0, jnp.where(j == 2, lf, lf + before)))
    return jnp.where(j == 0, lf + after, jnp.where(j == 1, lf, jnp.where(j == 2, 0.0, before)))


def _hgrn_gates(z, log2_lb, log2_1m_lb, one_m_lb, one):
    z2 = z * LOG2_E
    log2_sig = jnp.minimum(z2, 0.0) - jnp.log2(one + jnp.exp2(_neg_abs(z2)))
    b = log2_1m_lb + log2_sig
    log2_f = jnp.maximum(log2_lb, b) + jnp.log2(one + jnp.exp2(_neg_abs(log2_lb - b)))
    return log2_f, one_m_lb * jnp.exp2(log2_sig - z2)


def _cumsum_rows(cum_bf, x):
    w = x.shape[1]
    hi = x.astype(BF16)
    rest = x - hi.astype(F32)
    mid = rest.astype(BF16)
    lo = (rest - mid.astype(F32)).astype(BF16)
    y = _dot(cum_bf, jnp.concatenate([hi, mid, lo], axis=1))
    return y[:, :w] + y[:, w:2 * w] + y[:, 2 * w:]


def _hgrn_group(items, states, cum_ref, mask_ref):
    c = HGRN_CHUNK
    xs = [_cumsum_rows(cum_ref[d], log2_f) for _, _, _, log2_f, d in items]
    row = lax.broadcasted_iota(jnp.int32, xs[0].shape, 0)
    q_bf = [q.astype(BF16) for q, _, _, _, _ in items]
    k_bf = [k.astype(BF16) for _, k, _, _, _ in items]
    v_bf = [v.astype(BF16) for _, _, v, _, _ in items]
    masks = [mask_ref[d, 0] for d in (0, 1)]
    scores = [_dot_nt(q, k).astype(BF16) * masks[d] for q, k, (_, _, _, _, d) in zip(q_bf, k_bf, items)]
    for li, m in enumerate(_HGRN_LEVELS):
        masks = [mask_ref[d, li + 1] for d in (0, 1)]
        for j, (_, _, _, log2_f, d) in enumerate(items):
            e = jnp.exp2(_level_exponent(xs[j], log2_f, row, m, d == 0)).astype(BF16)
            scores[j] = scores[j] + _dot_nt(q_bf[j] * e, k_bf[j] * e).astype(BF16) * masks[d]
    outs = [_dot(sc, v) for sc, v in zip(scores, v_bf)]
    edges = [x[c - 1:c, :] if d == 0 else x[0:1, :] for x, (_, _, _, _, d) in zip(xs, items)]
    handed = [_dot_tn(v, (k * jnp.exp2(edge - x)).astype(BF16))
              for v, edge, x, (_, k, _, _, _) in zip(v_bf, edges, xs, items)]
    q_in = [(q * jnp.exp2(x)).astype(BF16) for x, (q, _, _, _, _) in zip(xs, items)]
    states = list(states)
    for j, (_, _, _, _, d) in enumerate(items):
        outs[j] = outs[j] + _dot_nt(q_in[j], states[d].astype(BF16))
        states[d] = states[d] * jnp.exp2(edges[j]) + handed[j]
    return outs, states


def _hgrn_body(q_ref, i_ref, zf_ref, zb_ref, g_ref, c_ref, cum_ref, mask_ref, o_ref, acc_ref):
    s, dk = q_ref.shape
    c = HGRN_CHUNK
    n = s // c
    consts = c_ref[...]
    acc_ref[...] = jnp.zeros_like(acc_ref)
    z_refs = (zf_ref, zb_ref)

    def step(gi, states):
        rows, items = [], []
        for u in range(HGRN_UNROLL):
            for d in (0, 1):
                ci = gi * HGRN_UNROLL + u
                r = pl.ds(pl.multiple_of((ci if d == 0 else n - 1 - ci) * c, c), c)
                log2_f, k = _hgrn_gates(z_refs[d][r, :], *(consts[3 * d + i:3 * d + i + 1] for i in range(3)),
                                        consts[7:8])
                rows.append(r)
                items.append((q_ref[r, :], k, i_ref[r, :], log2_f, d))
        outs, states = _hgrn_group(items, states, cum_ref, mask_ref)
        for r, o in zip(rows, outs):
            acc_ref[r, :] += o
        return tuple(states)

    zero = jnp.zeros((dk, dk), F32)
    lax.fori_loop(0, n // HGRN_UNROLL, step, (zero, zero), unroll=True)

    o = acc_ref[...]
    o = o * lax.rsqrt(jnp.mean(o * o, axis=-1, keepdims=True) + EPS) * consts[6:7]
    g = g_ref[...]
    o_ref[...] = (o * (g * jax.nn.sigmoid(g))).astype(o_ref.dtype)


def _hgrn(p_h, consts, cum, masks):
    b, s, _ = p_h.shape
    dk = HGRN_HEAD_DIM

    def part(p):
        return pl.BlockSpec((None, s, dk), lambda bi, hi, p=p: (bi, 0, p * HGRN_HEADS + hi))

    return pl.pallas_call(
        _hgrn_body,
        grid=(b, HGRN_HEADS),
        in_specs=[part(0), part(1), part(2), part(3), part(4),
                  pl.BlockSpec((8, dk), lambda bi, hi: (0, hi)),
                  _resident(cum.shape), _resident(masks.shape)],
        out_specs=pl.BlockSpec((None, s, dk), lambda bi, hi: (bi, 0, hi)),
        out_shape=jax.ShapeDtypeStruct((b, s, HGRN_WIDTH), BF16),
        scratch_shapes=[pltpu.VMEM((s, dk), F32)],
        compiler_params=pltpu.CompilerParams(dimension_semantics=("arbitrary", "arbitrary"),
                                             vmem_limit_bytes=VMEM_LIMIT),
        name="hgrn2",
    )(p_h, p_h, p_h, p_h, p_h, consts, cum, masks)


def _attn_geometry(s_len, dil):
    length = s_len // dil
    nq = min(ATTN_QBLK, length)
    nk = min(nq + 2 * ATTN_HALF, length)
    return length, nq, nk, length // nq


def _attn_body(x1_ref, x4_ref, x16_ref, sl_ref, o_ref, bias_s, m_s, num_s, den_s):
    x_refs = (x1_ref, x4_ref, x16_ref)
    s_len = x1_ref.shape[0]
    heads = LANES // ATTN_HEAD_DIM
    slab_w = ATTN_SLABS * LANES

    @pl.when(pl.program_id(1) == 0)
    def _():
        for di, dil in enumerate(ATTN_DILATIONS):
            _, nq, nk, _ = _attn_geometry(s_len, dil)
            rel0 = lax.broadcasted_iota(jnp.int32, (nq, nk), 1) - lax.broadcasted_iota(jnp.int32, (nq, nk), 0)
            for pos, offset in enumerate((0, -ATTN_HALF, nq - nk)):
                dist = jnp.abs(rel0 + offset).astype(F32)
                for h in range(heads):
                    pen = -(sl_ref[h:h + 1, :nk] * (float(dil) * dist))
                    bias_s[di, pos, h, :nq, :nk] = jnp.where(dist <= float(ATTN_HALF), pen, -jnp.inf)

    def slab(item, rows, i):
        di, _, _, r = item
        return x_refs[di][rows, r * slab_w + i * LANES: r * slab_w + (i + 1) * LANES]

    def rows_of(item):
        _, dil, bi, r = item
        length, nq, nk, n_blk = _attn_geometry(s_len, dil)
        q0 = bi * nq
        k0 = min(max(q0 - ATTN_HALF, 0), length - nk)
        pos = 0 if bi == 0 else (2 if bi == n_blk - 1 else 1)
        out_rows = pl.ds(r + dil * q0, nq, stride=dil) if dil > 1 else pl.ds(q0, nq)
        return pl.ds(q0, nq), pl.ds(k0, nk), out_rows, pos, nq, nk

    def scores(item):
        q_rows, k_rows, _, pos, nq, nk = rows_of(item)
        kb = slab(item, k_rows, _K)
        return [_dot_nt(slab(item, q_rows, (_Q0, _Q1)[h]), kb) + bias_s[item[0], pos, h, :nq, :nk]
                for h in range(heads)]

    def probs(s):
        m = jnp.max(s, axis=-1, keepdims=True)
        return m, jnp.exp(s - m).astype(BF16)

    def finish(item, stats):
        _, k_rows, out_rows, _, nq, _ = rows_of(item)
        head0 = lax.broadcasted_iota(jnp.int32, (nq, LANES), 1) < ATTN_HEAD_DIM
        acc = [_dot(p, slab(item, k_rows, (_V0, _V1)[h])) for h, (_, p) in enumerate(stats)]
        m_s[item[0], out_rows, :] = jnp.where(head0, stats[0][0], stats[1][0])
        num_s[item[0], out_rows, :] = jnp.where(head0, acc[0], acc[1])
        den_s[item[0], out_rows, :] = pltpu.roll(jnp.where(head0, acc[1], acc[0]), ATTN_HEAD_DIM, 1)

    items = [(di, dil, bi, r) for di, dil in enumerate(ATTN_DILATIONS)
             for bi in range(_attn_geometry(s_len, dil)[3]) for r in range(dil)]
    groups = [items[i:i + ATTN_BLOCKS_PER_BODY] for i in range(0, len(items), ATTN_BLOCKS_PER_BODY)]
    pending = None
    for group in groups + [None]:
        upcoming = [(item, scores(item)) for item in group] if group else None
        if pending:
            for item, ss in pending:
                finish(item, [probs(s) for s in ss])
        pending = upcoming

    rows_per_step = ATTN_QBLK

    def merge(ti, carry):
        rows = pl.ds(pl.multiple_of(ti * rows_per_step, rows_per_step), rows_per_step)
        ms = [m_s[di, rows, :] for di in range(len(ATTN_DILATIONS))]
        m_tot = functools.reduce(jnp.maximum, ms)
        ws = [jnp.exp(m - m_tot) for m in ms]
        num = sum(w * num_s[di, rows, :] for di, w in enumerate(ws))
        den = sum(w * den_s[di, rows, :] for di, w in enumerate(ws))
        o_ref[rows, :] = (num / den).astype(o_ref.dtype)
        return carry

    lax.fori_loop(0, s_len // rows_per_step, merge, 0)


def _attn(p_views, slopes):
    b, pairs, s, slab_w = p_views[0].shape
    heads = LANES // ATTN_HEAD_DIM
    n_dil = len(ATTN_DILATIONS)
    nk_max = ATTN_QBLK + 2 * ATTN_HALF

    def view_spec(dil):
        return pl.BlockSpec((None, None, s // dil, dil * slab_w), lambda hi, bi: (bi, hi, 0, 0))

    return pl.pallas_call(
        _attn_body,
        grid=(pairs, b),
        in_specs=[view_spec(d) for d in ATTN_DILATIONS]
                 + [pl.BlockSpec((None, 8, nk_max), lambda hi, bi: (hi, 0, 0))],
        out_specs=pl.BlockSpec((None, s, LANES), lambda hi, bi: (bi, 0, hi)),
        out_shape=jax.ShapeDtypeStruct((b, s, ATTN_WIDTH), BF16),
        scratch_shapes=[pltpu.VMEM((n_dil, 3, heads, ATTN_QBLK, nk_max), F32),
                        pltpu.VMEM((n_dil, s, LANES), F32),
                        pltpu.VMEM((n_dil, s, LANES), F32),
                        pltpu.VMEM((n_dil, s, LANES), F32)],
        compiler_params=pltpu.CompilerParams(dimension_semantics=("arbitrary", "arbitrary"),
                                             vmem_limit_bytes=VMEM_LIMIT),
        name="dilated_attn",
    )(*p_views, slopes)


def _memkv_body(m_ref, g_ref, w_ref, k_ref, v_ref):
    mn = _rms(m_ref[...], g_ref[...]).astype(BF16)
    d = k_ref.shape[1]
    k_ref[...] = _dot(mn, w_ref[:, :d]).astype(BF16)
    v_ref[...] = _dot(mn, w_ref[:, d:]).astype(BF16)


def _memkv(mem, g, w_kv):
    b, m, d = mem.shape
    blk = pl.BlockSpec((None, m, d), lambda bi: (bi, 0, 0))
    return pl.pallas_call(
        _memkv_body,
        grid=(b,),
        in_specs=[blk, _resident((1, d)), _resident(w_kv.shape)],
        out_specs=[blk, blk],
        out_shape=[jax.ShapeDtypeStruct((b, m, d), BF16)] * 2,
        compiler_params=pltpu.CompilerParams(dimension_semantics=("arbitrary",), vmem_limit_bytes=VMEM_LIMIT),
        name="mem_kv",
    )(mem, g, w_kv)


def _mix_xattn_body(h_ref, yh_ref, ya_ref, wo_h_ref, wo_a_ref, g_ref, wq_ref, k_ref, v_ref, wxo_ref, o_ref):
    h = h_ref[...] + _dot(yh_ref[...], wo_h_ref[...]) + _dot(ya_ref[...], wo_a_ref[...])
    d = h.shape[1]
    dh = d // MEM_HEADS
    hn = _rms(h, g_ref[...]).astype(BF16)
    q = (_dot(hn, wq_ref[...]) * (1.0 / math.sqrt(dh))).astype(BF16)
    outs = []
    for hd in range(MEM_HEADS):
        cols = slice(hd * dh, (hd + 1) * dh)
        s = _dot_nt(q[:, cols], k_ref[:, cols])
        p = jnp.exp(s - jnp.max(s, axis=-1, keepdims=True))
        l = jnp.sum(p, axis=-1, keepdims=True)
        outs.append((_dot(p.astype(BF16), v_ref[:, cols]) / l).astype(BF16))
    o_ref[...] = h + _dot(jnp.concatenate(outs, axis=1), wxo_ref[...])


def _mix_xattn(h, y_h, y_a, wo_h, wo_a, g, wq, k, v, wxo):
    b, s, d = h.shape
    tm = DENSE_TOKEN_TILE
    m = k.shape[1]

    def rows(w):
        return pl.BlockSpec((None, tm, w), lambda bi, i: (bi, i, 0))

    mem = pl.BlockSpec((None, m, d), lambda bi, i: (bi, 0, 0))
    return pl.pallas_call(
        _mix_xattn_body,
        grid=(b, s // tm),
        in_specs=[rows(d), rows(y_h.shape[2]), rows(y_a.shape[2]), _resident(wo_h.shape), _resident(wo_a.shape),
                  _resident((1, d)), _resident(wq.shape), mem, mem, _resident(wxo.shape)],
        out_specs=rows(d),
        out_shape=jax.ShapeDtypeStruct((b, s, d), F32),
        compiler_params=pltpu.CompilerParams(dimension_semantics=("arbitrary", "arbitrary"),
                                             vmem_limit_bytes=VMEM_LIMIT),
        name="mix_out_xattn",
    )(h, y_h, y_a, wo_h, wo_a, g, wq, k, v, wxo)


def _hgrn_consts(lb_fwd, lb_bwd, out_gain):
    rows = []
    for lb in (lb_fwd, lb_bwd):
        rows += [jnp.log(lb) * LOG2_E, jnp.log1p(-lb) * LOG2_E, 1.0 - lb]
    rows += [out_gain, jnp.ones_like(out_gain)]
    return jnp.stack(rows).astype(F32)


def _attn_slopes():
    sl = np.array([2.0 ** (-8.0 * (h + 1) / ATTN_HEADS) for h in range(ATTN_HEADS)], np.float32)
    per_pair = LANES // ATTN_HEAD_DIM
    out = np.zeros((ATTN_HEADS // per_pair, 8, ATTN_QBLK + 2 * ATTN_HALF), np.float32)
    for h in range(ATTN_HEADS):
        out[h // per_pair, h % per_pair, :] = sl[h]
    return jnp.asarray(out)


def kernel(x, mem, ln_ffn1, ffn1_w_gate, ffn1_w_up, ffn1_w_down, ln_mix, w_in, hgrn_lb_logits, hgrn_out_norm,
           w_out, ln_xq, ln_mem, w_xq, w_xkv, w_xo, ln_ffn2, ffn2_w_gate, ffn2_w_up, ffn2_w_down, ln_final):
    b, s, d = x.shape
    depth = ln_ffn1.shape[0]
    assert s % (ATTN_DILATIONS[-1] * 2 * ATTN_HALF) == 0 and s % ATTN_QBLK == 0
    assert s % (HGRN_CHUNK * HGRN_UNROLL) == 0 and s % TOKEN_TILE == 0 and s % DENSE_TOKEN_TILE == 0

    lb_all = jnp.cumsum(jax.nn.softmax(hgrn_lb_logits.astype(F32), axis=0), axis=0)
    lb_all = lb_all - lb_all[0:1]
    cum, masks = _hgrn_tables(HGRN_CHUNK)
    slopes = _attn_slopes()
    bf = lambda w: w.astype(BF16)
    vec = lambda g: g.astype(F32).reshape(1, -1)

    h = x.reshape(b * s, d)
    for l in range(depth):
        h = _ffn(h, vec(ln_ffn1[l]), bf(ffn1_w_gate[l]), bf(ffn1_w_up[l]), bf(ffn1_w_down[l]))
        p_h, *p_views = _proj(h.reshape(b, s, d), vec(ln_mix[l]), bf(w_in[l]))
        y_h = _hgrn(p_h, _hgrn_consts(lb_all[l, 0], lb_all[l, 1], hgrn_out_norm[l]), cum, masks)
        y_a = _attn(p_views, slopes)
        k_mem, v_mem = _memkv(mem, vec(ln_mem[l]), bf(w_xkv[l]))
        wo = bf(w_out[l])
        h = _mix_xattn(h.reshape(b, s, d), y_h, y_a, wo[:HGRN_WIDTH], wo[HGRN_WIDTH:], vec(ln_xq[l]),
                       bf(w_xq[l]), k_mem, v_mem, bf(w_xo[l])).reshape(b * s, d)
        last = l == depth - 1
        h = _ffn(h, vec(ln_ffn2[l]), bf(ffn2_w_gate[l]), bf(ffn2_w_up[l]), bf(ffn2_w_down[l]),
                 vec(ln_final) if last else None)
    return h.reshape(b, s, d)
```
